```python
import jax, jax.numpy as jnp
from jax import lax
import numpy as np

D_MODEL = 1024
BATCH = 8
SEQ = 4096
DEPTH = 1

CHUNK = 64
EPS = 1e-6
GLA_HEADS = 4
GLA_DK = 128
GLA_DV = 128
GLA_LOWRANK = 16
GLA_TAU = 16.0
GLA_QK = GLA_HEADS * GLA_DK
GLA_V = GLA_HEADS * GLA_DV
SGU_GROUPS = 4
SGU_GROUP_DIM = 128
SGU_BLOCK = 128
SGU_W = SGU_GROUPS * SGU_GROUP_DIM
PEER_HEADS = 8
PEER_DKEY = 256
PEER_NKEYS = 128
PEER_TOPK = 16
PEER_EXPERTS = PEER_NKEYS * PEER_NKEYS
PEER_TOKEN_BLOCK = 128
IN_COLS = (GLA_QK, GLA_QK, GLA_V, GLA_V, GLA_LOWRANK, SGU_W, SGU_W, D_MODEL, D_MODEL)
IN_WIDTH = sum(IN_COLS)

kernel_name = "hybrid_gla_sgu_peer_block"


def rmsnorm(x, g):
    xf = x.astype(jnp.float32)
    y = xf * lax.rsqrt(jnp.mean(xf * xf, axis=-1, keepdims=True) + EPS)
    return (y * g.astype(jnp.float32)).astype(x.dtype)


def gla_branch(q, k, v, r, a_lr, w_a2, b_a, g_out):
    B, S, _ = q.shape
    NC = S // CHUNK
    f32 = jnp.float32
    qc = q.astype(f32).reshape(B, NC, CHUNK, GLA_HEADS, GLA_DK) * (GLA_DK ** -0.5)
    kc = k.astype(f32).reshape(B, NC, CHUNK, GLA_HEADS, GLA_DK)
    vc = v.astype(f32).reshape(B, NC, CHUNK, GLA_HEADS, GLA_DV)
    log_a = jax.nn.log_sigmoid((a_lr @ w_a2 + b_a).astype(f32)) / GLA_TAU
    log_a = log_a.reshape(B, NC, CHUNK, GLA_HEADS, GLA_DK)
    cum = jnp.cumsum(log_a, axis=2)
    total = cum[:, :, -1]
    k_dec = kc * jnp.exp(total[:, :, None] - cum)
    chunk_kv = jnp.einsum('bnchk,bnchv->bnhkv', k_dec, vc)
    decay = jnp.exp(total)

    def step(state, inp):
        dec, kv, qq = inp
        state = dec[..., None] * state + kv
        out = jnp.einsum('bchk,bhkv->bchv', qq, state)
        return state, out

    init = jnp.zeros((B, GLA_HEADS, GLA_DK, GLA_DV), f32)
    _, o = lax.scan(step, init, (jnp.moveaxis(decay, 1, 0), jnp.moveaxis(chunk_kv, 1, 0),
                                 jnp.moveaxis(qc, 1, 0)))
    o = jnp.moveaxis(o, 0, 1).reshape(B, S, GLA_HEADS, GLA_DV)
    o = o * lax.rsqrt(jnp.mean(o * o, axis=-1, keepdims=True) + EPS)
    o = o.reshape(B, S, GLA_V) * g_out.astype(f32)
    return (o * jax.nn.silu(r.astype(f32))).astype(q.dtype)


def sgu_branch(u, v, ln_g, ln_b, w_s, b_s):
    B, S, _ = v.shape
    f32 = jnp.float32
    vf = v.astype(f32)
    mu = jnp.mean(vf, axis=-1, keepdims=True)
    var = jnp.mean(jnp.square(vf - mu), axis=-1, keepdims=True)
    vn = (vf - mu) * lax.rsqrt(var + EPS) * ln_g.astype(f32) + ln_b.astype(f32)
    NB = S // SGU_BLOCK
    vn = vn.reshape(B, NB, SGU_BLOCK, SGU_GROUPS, SGU_GROUP_DIM)
    pos = jnp.arange(SGU_BLOCK) // CHUNK
    mask = pos[:, None] >= pos[None, :]
    w = jnp.where(mask[None], w_s.astype(f32), 0.0)
    mixed = jnp.einsum('gij,bnjgc->bnigc', w, vn) + b_s.astype(f32).T[None, None, :, :, None]
    return (u.astype(f32) * mixed.reshape(B, S, SGU_W)).astype(u.dtype)


def peer(x, w_q, sub_k1, sub_k2, expert_u, expert_v):
    B, S, D = x.shape
    T = B * S
    f32 = jnp.float32
    xt = x.reshape(T, D)
    q = (xt @ w_q).astype(f32).reshape(T, PEER_HEADS, 2, PEER_DKEY // 2)
    s1 = jnp.einsum('thd,kd->thk', q[:, :, 0], sub_k1.astype(f32))
    s2 = jnp.einsum('thd,kd->thk', q[:, :, 1], sub_k2.astype(f32))
    v1, i1 = lax.top_k(s1, PEER_TOPK)
    v2, i2 = lax.top_k(s2, PEER_TOPK)
    cand = (v1[..., :, None] + v2[..., None, :]).reshape(T, PEER_HEADS, PEER_TOPK * PEER_TOPK)
    vals, ci = lax.top_k(cand, PEER_TOPK)
    ia = ci // PEER_TOPK
    ib = ci % PEER_TOPK
    idx = jnp.take_along_axis(i1, ia, axis=-1) * PEER_NKEYS + jnp.take_along_axis(i2, ib, axis=-1)
    gates = jax.nn.softmax(vals, axis=-1)
    NBLK = T // PEER_TOKEN_BLOCK
    HK = PEER_HEADS * PEER_TOPK
    idx = idx.reshape(NBLK, PEER_TOKEN_BLOCK, HK)
    gates = gates.reshape(NBLK, PEER_TOKEN_BLOCK, HK)
    xb = xt.reshape(NBLK, PEER_TOKEN_BLOCK, D)

    def block(args):
        xx, ii, gg = args
        u = expert_u[ii]
        h = jnp.einsum('tkd,td->tk', u, xx).astype(f32)
        a = (jax.nn.gelu(h) * gg).astype(x.dtype)
        return jnp.einsum('tk,tkd->td', a, expert_v[ii])

    out = lax.map(block, (xb, idx, gates))
    return out.reshape(B, S, D)


def setup_inputs(seed: int = 0) -> dict:
    key = jax.random.key(seed)
    ks = jax.random.split(key, 24)
    f32 = jnp.float32
    L, D = DEPTH, D_MODEL
    nrm = lambda k, shape, scale: jax.random.normal(k, shape, f32) * scale
    return {
        "x": jax.random.normal(ks[0], (BATCH, SEQ, D), f32),
        "norm1_g": 1.0 + nrm(ks[1], (L, D), 0.02),
        "w_in": nrm(ks[2], (L, D, IN_WIDTH), D ** -0.5),
        "w_gate_up": nrm(ks[3], (L, GLA_LOWRANK, GLA_QK), GLA_LOWRANK ** -0.5),
        "b_gate": 1.0 + nrm(ks[4], (L, GLA_QK), 0.1),
        "gla_norm_g": 1.0 + nrm(ks[5], (L, GLA_V), 0.02),
        "sgu_ln_g": 1.0 + nrm(ks[6], (L, SGU_W), 0.02),
        "sgu_ln_b": nrm(ks[7], (L, SGU_W), 0.02),
        "sgu_w": nrm(ks[8], (L, SGU_GROUPS, SGU_BLOCK, SGU_BLOCK), 0.5 * SGU_BLOCK ** -0.5),
        "sgu_b": 1.0 + nrm(ks[9], (L, SGU_GROUPS, SGU_BLOCK), 0.02),
        "w_branch_a": nrm(ks[10], (L, GLA_V, D), GLA_V ** -0.5),
        "w_branch_b": nrm(ks[11], (L, SGU_W, D), SGU_W ** -0.5),
        "w_out": nrm(ks[12], (L, D, D), D ** -0.5),
        "norm2_g": 1.0 + nrm(ks[13], (L, D), 0.02),
        "peer_wq": nrm(ks[14], (L, D, PEER_HEADS * PEER_DKEY), D ** -0.5),
        "peer_k1": nrm(ks[15], (L, PEER_NKEYS, PEER_DKEY // 2), (PEER_DKEY // 2) ** -0.5),
        "peer_k2": nrm(ks[16], (L, PEER_NKEYS, PEER_DKEY // 2), (PEER_DKEY // 2) ** -0.5),
        "peer_u": nrm(ks[17], (L, PEER_EXPERTS, D), D ** -0.5),
        "peer_v": nrm(ks[18], (L, PEER_EXPERTS, D), PEER_HEADS ** -0.5),
        "final_g": 1.0 + nrm(ks[19], (D,), 0.02),
    }


def reference(x, norm1_g, w_in, w_gate_up, b_gate, gla_norm_g, sgu_ln_g, sgu_ln_b, sgu_w, sgu_b,
              w_branch_a, w_branch_b, w_out, norm2_g, peer_wq, peer_k1, peer_k2, peer_u, peer_v,
              final_g):
    offs = np.cumsum(IN_COLS)[:-1].tolist()
    h = x
    for l in range(DEPTH):
        n1 = rmsnorm(h, norm1_g[l])
        proj = n1 @ w_in[l]
        q, k, v, r, a_lr, su, sv, ga, gb = jnp.split(proj, offs, axis=-1)
        y_a = gla_branch(q, k, v, r, a_lr, w_gate_up[l], b_gate[l], gla_norm_g[l])
        su = jax.nn.gelu(su)
        sv = jax.nn.gelu(sv)
        y_b = sgu_branch(su, sv, sgu_ln_g[l], sgu_ln_b[l], sgu_w[l], sgu_b[l])
        merged = (jax.nn.sigmoid(ga) * (y_a @ w_branch_a[l])
                  + jax.nn.sigmoid(gb) * (y_b @ w_branch_b[l]))
        h = h + merged @ w_out[l]
        n2 = rmsnorm(h, norm2_g[l])
        h = h + peer(n2, peer_wq[l], peer_k1[l], peer_k2[l], peer_u[l], peer_v[l])
    return rmsnorm(h, final_g)
```

```python
import functools
import math

import jax
import jax.numpy as jnp
import numpy as np
from jax import lax
from jax.experimental import pallas as pl
from jax.experimental.pallas import tpu as pltpu

F32 = jnp.float32
BF16 = jnp.bfloat16

D_MODEL = 1024
BATCH = 8
SEQ = 4096
TOKENS = BATCH * SEQ
CHUNK = 64
EPS = 1e-6
GLA_HEADS = 4
GLA_DK = 128
GLA_DV = 128
GLA_LOWRANK = 16
GLA_TAU = 16.0
GLA_W = GLA_HEADS * GLA_DK
SGU_GROUPS = 4
SGU_BLOCK = 128
SGU_W = 512
PEER_HEADS = 8
PEER_DKEY = 256
PEER_NKEYS = 128
PEER_TOPK = 16
PEER_EXPERTS = PEER_NKEYS * PEER_NKEYS

V7X_LANES = 128
V7X_SUBLANES = 8
V7X_BF16_ROWS = 16
V7X_VMEM_BYTES = 64 * 1024 * 1024

_C_QKVR = 0
_C_SU = 2048
_C_SV = 2560
_C_GA = 3072
_C_GB = 4096
_C_ALR = 5120
_IN_PACKED = 5248

MIX_SB = 256
SEL_TB = 256
DENSE_TB = 512
DENSE_EB = 2048
DENSE_HC = 512

_NT = (((1,), (1,)), ((), ()))
_TN = (((0,), (0,)), ((), ()))


def _dot(a, b):
    return jnp.dot(a, b, preferred_element_type=F32)


def _dot_nt(a, b):
    return lax.dot_general(a, b, _NT, preferred_element_type=F32)


def _vmem_limit(block_bytes):
    return int(min(V7X_VMEM_BYTES - 8 * 1024 * 1024, 2 * block_bytes + 16 * 1024 * 1024))


def _oddeven_merge_sort_pairs(n):
    pairs = []
    p = 1
    while p < n:
        k = p
        while k >= 1:
            for j in range(k % p, n - k, 2 * k):
                for i in range(min(k, n - j - k)):
                    if (i + j) // (2 * p) == (i + j + k) // (2 * p):
                        pairs.append((i + j, i + j + k))
            k //= 2
        p *= 2
    return pairs


_SORT16 = _oddeven_merge_sort_pairs(16)
_CLEAN16 = [(i, i + d) for d in (8, 4, 2, 1) for i in range(16) if not i & d]


def _cmpx(xs, i, j):
    a, b = xs[i], xs[j]
    if b is None:
        return
    if a is None:
        xs[i], xs[j] = b, None
        return
    xs[i] = jnp.maximum(a, b)
    xs[j] = jnp.minimum(a, b)


def _sort16(xs):
    xs = list(xs)
    for i, j in _SORT16:
        _cmpx(xs, i, j)
    return xs


def _merge_top16(a, b):
    z = []
    for k in range(16):
        x, y = a[k], b[15 - k]
        if y is None:
            z.append(x)
        elif x is None:
            z.append(y)
        else:
            z.append(jnp.maximum(x, y))
    for i, j in _CLEAN16:
        _cmpx(z, i, j)
    return z


def _top16_of(vals):
    vals = list(vals) + [None] * (-len(vals) % 16)
    groups = [_sort16(vals[g:g + 16]) for g in range(0, len(vals), 16)]
    while len(groups) > 1:
        nxt = [_merge_top16(groups[g], groups[g + 1]) for g in range(0, len(groups) - 1, 2)]
        if len(groups) % 2:
            nxt.append(groups[-1])
        groups = nxt
    return groups[0]


def _top16_rows(tiles):
    xs = _sort16(tiles)
    for shift in (4, 2, 1):
        rolled = [pltpu.roll(x, shift, 0) for x in xs]
        xs = _merge_top16(xs, rolled)
    return xs


def _fold_kernel(k_ref, wq_ref, o_ref):
    o_ref[...] = lax.dot_general(k_ref[0], wq_ref[...], _NT,
                                 precision=lax.Precision.HIGHEST,
                                 preferred_element_type=F32).astype(o_ref.dtype)


def _peer_fold(k12, wq):
    half = PEER_DKEY // 2
    return pl.pallas_call(
        _fold_kernel,
        grid=(2, PEER_HEADS),
        in_specs=[
            pl.BlockSpec((1, PEER_NKEYS, half), lambda s, h: (s, 0, 0)),
            pl.BlockSpec((D_MODEL, half), lambda s, h: (0, 2 * h + s)),
        ],
        out_specs=pl.BlockSpec((PEER_NKEYS, D_MODEL), lambda s, h: (s * PEER_HEADS + h, 0)),
        out_shape=jax.ShapeDtypeStruct((2 * PEER_HEADS * PEER_NKEYS, D_MODEL), BF16),
        name="peer_fold",
    )(k12, wq)


def _mix_kernel(x_ref, g1_ref, win_ref, wa2_ref, bg_ref, gno_ref, lng_ref, lnb_ref,
                sw_ref, sbb_ref, wba_ref, wbb_ref, wout_ref, g2_ref,
                h_ref, n2_ref, st_ref, ya_ref, yb_ref):
    sb = MIX_SB

    @pl.when(pl.program_id(1) == 0)
    def _():
        st_ref[...] = jnp.zeros_like(st_ref)

    x = x_ref[0]
    n1 = x * lax.rsqrt(jnp.mean(x * x, axis=-1, keepdims=True) + EPS) * g1_ref[...]
    n1b = n1.astype(BF16)

    qkvr = _dot(n1b, win_ref[:, _C_QKVR:_C_QKVR + 4 * GLA_W])
    q = qkvr[:, 0:GLA_W] * (GLA_DK ** -0.5)
    k = qkvr[:, GLA_W:2 * GLA_W]
    v = qkvr[:, 2 * GLA_W:3 * GLA_W]
    r = qkvr[:, 3 * GLA_W:4 * GLA_W]
    alr = _dot(n1b, win_ref[:, _C_ALR:_C_ALR + V7X_LANES])
    z = _dot(alr.astype(BF16), wa2_ref[...]) + bg_ref[...]
    la = (jnp.minimum(z, 0.0) - jnp.log1p(jnp.exp(-jnp.abs(z)))) * (1.0 / GLA_TAU)
    ri = lax.broadcasted_iota(jnp.int32, (sb, sb), 0)
    ci = lax.broadcasted_iota(jnp.int32, (sb, sb), 1)
    later = jnp.where((ci > ri) & ((ci // CHUNK) == (ri // CHUNK)), 1.0, 0.0).astype(BF16)
    la_hi = la.astype(BF16)
    la_lo = (la - la_hi.astype(F32)).astype(BF16)
    dec = _dot(later, la_hi) + _dot(later, la_lo)
    kdec = (k * jnp.exp(dec)).astype(BF16)
    vb = v.astype(BF16)
    qb = q.astype(BF16)
    for hh in range(GLA_HEADS):
        ls = slice(hh * GLA_DK, (hh + 1) * GLA_DK)
        st = st_ref[hh]
        for c in range(sb // CHUNK):
            rs = slice(c * CHUNK, (c + 1) * CHUNK)
            tot = dec[c * CHUNK:c * CHUNK + 1, ls] + la[c * CHUNK:c * CHUNK + 1, ls]
            kv_t = lax.dot_general(vb[rs, ls], kdec[rs, ls], _TN, preferred_element_type=F32)
            st = st * jnp.exp(tot) + kv_t
            o = _dot_nt(qb[rs, ls], st.astype(BF16))
            o = o * lax.rsqrt(jnp.mean(o * o, axis=-1, keepdims=True) + EPS)
            ya_ref[rs, ls] = o
        st_ref[hh] = st
    y_a = ya_ref[...] * gno_ref[...] * (r * jax.nn.sigmoid(r))

    su = jax.nn.gelu(_dot(n1b, win_ref[:, _C_SU:_C_SU + SGU_W]))
    sv = jax.nn.gelu(_dot(n1b, win_ref[:, _C_SV:_C_SV + SGU_W]))
    mu = jnp.mean(sv, axis=-1, keepdims=True)
    svc = sv - mu
    var = jnp.mean(svc * svc, axis=-1, keepdims=True)
    vn = (svc * lax.rsqrt(var + EPS) * lng_ref[...] + lnb_ref[...]).astype(BF16)
    bi = lax.broadcasted_iota(jnp.int32, (SGU_BLOCK, SGU_BLOCK), 0)
    bj = lax.broadcasted_iota(jnp.int32, (SGU_BLOCK, SGU_BLOCK), 1)
    causal = (bi // CHUNK) >= (bj // CHUNK)
    for g in range(SGU_GROUPS):
        ls = slice(g * SGU_BLOCK, (g + 1) * SGU_BLOCK)
        wm = jnp.where(causal, sw_ref[g], 0.0).astype(BF16)
        for nb in range(sb // SGU_BLOCK):
            rs = slice(nb * SGU_BLOCK, (nb + 1) * SGU_BLOCK)
            mixed = _dot(wm, vn[rs, ls]) + sbb_ref[g]
            yb_ref[rs, ls] = su[rs, ls] * mixed
    y_b = yb_ref[...]

    ga = jax.nn.sigmoid(_dot(n1b, win_ref[:, _C_GA:_C_GA + D_MODEL]))
    gb = jax.nn.sigmoid(_dot(n1b, win_ref[:, _C_GB:_C_GB + D_MODEL]))
    merged = ga * _dot(y_a.astype(BF16), wba_ref[...]) + gb * _dot(y_b.astype(BF16), wbb_ref[...])
    h = x + _dot(merged.astype(BF16), wout_ref[...])
    h_ref[0] = h
    n2 = h * lax.rsqrt(jnp.mean(h * h, axis=-1, keepdims=True) + EPS) * g2_ref[...]
    n2_ref[0] = n2.astype(BF16)


def _mixers(x, g1, win, wa2, bg, gno, lng, lnb, sw, sbb, wba, wbb, wout, g2):
    sb = MIX_SB
    full = lambda a: pl.BlockSpec(a.shape, lambda b, s, _n=a.ndim: (0,) * _n)
    consts = (g1, win, wa2, bg, gno, lng, lnb, sw, sbb, wba, wbb, wout, g2)
    blk = sum(int(np.prod(a.shape)) * a.dtype.itemsize for a in consts)
    blk += sb * D_MODEL * (4 + 4 + 2)
    return pl.pallas_call(
        _mix_kernel,
        grid=(BATCH, SEQ // sb),
        in_specs=[pl.BlockSpec((1, sb, D_MODEL), lambda b, s: (b, s, 0))] + [full(a) for a in consts],
        out_specs=[pl.BlockSpec((1, sb, D_MODEL), lambda b, s: (b, s, 0)),
                   pl.BlockSpec((1, sb, D_MODEL), lambda b, s: (b, s, 0))],
        out_shape=[jax.ShapeDtypeStruct((BATCH, SEQ, D_MODEL), F32),
                   jax.ShapeDtypeStruct((BATCH, SEQ, D_MODEL), BF16)],
        scratch_shapes=[pltpu.VMEM((GLA_HEADS, GLA_DV, GLA_DK), F32),
                        pltpu.VMEM((sb, GLA_W), F32),
                        pltpu.VMEM((sb, SGU_W), F32)],
        compiler_params=pltpu.CompilerParams(
            dimension_semantics=("arbitrary", "arbitrary"),
            vmem_limit_bytes=_vmem_limit(blk)),
        name="mixers",
    )(x, *consts)


def _row(ref, idx, h, rows):
    return jnp.broadcast_to(ref[idx, pl.ds(h, 1), :], (rows, ref.shape[-1]))


def _select_kernel(n2_ref, w1_ref, w2_ref, p_ref, c_ref, q_ref, r_ref,
                   s1_ref, s2_ref, v1_ref, v2_ref, aux_ref):
    tb = SEL_TB
    n2 = n2_ref[...]
    s1_ref[...] = _dot_nt(w1_ref[...], n2)
    s2_ref[...] = _dot_nt(w2_ref[...], n2)

    def top_body(h, carry):
        base = pl.multiple_of(h * PEER_NKEYS, PEER_NKEYS)
        for s_ref, v_ref in ((s1_ref, v1_ref), (s2_ref, v2_ref)):
            tiles = [s_ref[pl.ds(base + V7X_SUBLANES * t, V7X_SUBLANES), :]
                     for t in range(PEER_NKEYS // V7X_SUBLANES)]
            top = _top16_rows(tiles)
            for i in range(PEER_TOPK):
                v_ref[i, pl.ds(h, 1), :] = top[i][0:1, :]
        return carry

    lax.fori_loop(0, PEER_HEADS, top_body, 0)

    v1 = [v1_ref[i] for i in range(PEER_TOPK)]
    v2 = [v2_ref[i] for i in range(PEER_TOPK)]
    cands = [v1[i] + v2[j] for i in range(PEER_TOPK) for j in range(PEER_TOPK)
             if (i + 1) * (j + 1) <= PEER_TOPK]
    thr = _top16_of(cands)[PEER_TOPK - 1]
    cmax = v1[0] + v2[0]
    zsum = jnp.zeros((PEER_HEADS, tb), F32)
    for cnd in cands:
        zsum = zsum + jnp.where(cnd >= thr, jnp.exp(cnd - cmax), 0.0)
    aux_ref[0] = thr
    aux_ref[1] = 1.0 / zsum

    def emit_body(h, carry):
        base = pl.multiple_of(h * PEER_NKEYS, PEER_NKEYS)
        v2b = [_row(v2_ref, j, h, V7X_BF16_ROWS) for j in range(PEER_TOPK)]
        thr_b = _row(aux_ref, 0, h, V7X_BF16_ROWS)
        iz_b = _row(aux_ref, 1, h, V7X_BF16_ROWS)
        m1_b = _row(v1_ref, 0, h, V7X_BF16_ROWS)
        m2_b = v2b[0]
        for t in range(PEER_NKEYS // V7X_BF16_ROWS):
            rs = pl.ds(base + V7X_BF16_ROWS * t, V7X_BF16_ROWS)
            os = pl.ds(V7X_BF16_ROWS * t, V7X_BF16_ROWS)
            s1t = s1_ref[rs, :]
            cnt = jnp.zeros_like(s1t)
            for j in range(PEER_TOPK):
                cnt = cnt + jnp.where(s1t + v2b[j] >= thr_b, 1.0, 0.0)
            c_ref[h, os, :] = cnt
            p_ref[h, os, :] = jnp.exp(s1t - m1_b)
            s2t = s2_ref[rs, :]
            rank = jnp.zeros_like(s2t)
            for j in range(PEER_TOPK):
                rank = rank + jnp.where(v2b[j] > s2t, 1.0, 0.0)
            r_ref[h, os, :] = rank.astype(BF16)
            q_ref[h, os, :] = (jnp.exp(s2t - m2_b) * iz_b).astype(BF16)
        return carry

    lax.fori_loop(0, PEER_HEADS, emit_body, 0)


def _peer_select(n2, w1, w2):
    tb = SEL_TB
    hk = PEER_HEADS * PEER_NKEYS
    sel = lambda dt: jax.ShapeDtypeStruct((PEER_HEADS, PEER_NKEYS, TOKENS), dt)
    ospec = pl.BlockSpec((PEER_HEADS, PEER_NKEYS, tb), lambda i: (0, 0, i))
    blk = tb * D_MODEL * 2 + 2 * hk * D_MODEL * 2 + hk * tb * (4 + 4 + 2 + 2)
    return pl.pallas_call(
        _select_kernel,
        grid=(TOKENS // tb,),
        in_specs=[pl.BlockSpec((tb, D_MODEL), lambda i: (i, 0)),
                  pl.BlockSpec((hk, D_MODEL), lambda i: (0, 0)),
                  pl.BlockSpec((hk, D_MODEL), lambda i: (0, 0))],
        out_specs=[ospec, ospec, ospec, ospec],
        out_shape=[sel(F32), sel(F32), sel(BF16), sel(BF16)],
        scratch_shapes=[pltpu.VMEM((hk, tb), F32), pltpu.VMEM((hk, tb), F32),
                        pltpu.VMEM((PEER_TOPK, PEER_HEADS, tb), F32),
                        pltpu.VMEM((PEER_TOPK, PEER_HEADS, tb), F32),
                        pltpu.VMEM((2, PEER_HEADS, tb), F32)],
        compiler_params=pltpu.CompilerParams(
            dimension_semantics=("arbitrary",),
            vmem_limit_bytes=_vmem_limit(blk)),
        name="peer_select",
    )(n2, w1, w2)


def _dense_kernel(n2_ref, u_ref, vt_ref, p_ref, c_ref, q_ref, r_ref, h_ref, fg_ref,
                  y_ref, acc_ref, at_ref):
    tb = DENSE_TB
    eb = pl.program_id(1)

    @pl.when(eb == 0)
    def _():
        acc_ref[...] = jnp.zeros_like(acc_ref)

    n2 = n2_ref[...]
    keys_per_chunk = DENSE_HC // PEER_NKEYS
    for hc in range(DENSE_EB // DENSE_HC):
        ht = _dot_nt(u_ref[hc * DENSE_HC:(hc + 1) * DENSE_HC, :], n2)
        for al in range(keys_per_chunk):
            a = hc * keys_per_chunk + al
            act = jax.nn.gelu(ht[al * PEER_NKEYS:(al + 1) * PEER_NKEYS, :].astype(BF16))
            gate = None
            for h in range(PEER_HEADS):
                prow = p_ref[h, a:a + 1, :].astype(BF16)
                crow = c_ref[h, a:a + 1, :].astype(BF16)
                term = jnp.where(r_ref[h] < crow, prow * q_ref[h], jnp.zeros((), BF16))
                gate = term if gate is None else gate + term
            at_ref[a * PEER_NKEYS:(a + 1) * PEER_NKEYS, :] = act * gate
    acc_ref[...] += _dot(vt_ref[...], at_ref[...])

    @pl.when(eb == pl.num_programs(1) - 1)
    def _():
        h2 = h_ref[...] + acc_ref[...].T
        y = h2 * lax.rsqrt(jnp.mean(h2 * h2, axis=-1, keepdims=True) + EPS) * fg_ref[...]
        y_ref[...] = y


def _peer_dense(n2, u, vt, p, c, q, r, h, fg):
    tb, eb = DENSE_TB, DENSE_EB
    keys = eb // PEER_NKEYS
    blk = (tb * D_MODEL * 2 + 2 * eb * D_MODEL * 2 + 2 * PEER_HEADS * keys * tb * 4
           + 2 * PEER_HEADS * PEER_NKEYS * tb * 2 + 2 * tb * D_MODEL * 4)
    scratch = D_MODEL * tb * 4 + eb * tb * 2
    return pl.pallas_call(
        _dense_kernel,
        grid=(TOKENS // tb, PEER_EXPERTS // eb),
        in_specs=[
            pl.BlockSpec((tb, D_MODEL), lambda t, e: (t, 0)),
            pl.BlockSpec((eb, D_MODEL), lambda t, e: (e, 0)),
            pl.BlockSpec((D_MODEL, eb), lambda t, e: (0, e)),
            pl.BlockSpec((PEER_HEADS, keys, tb), lambda t, e: (0, e, t)),
            pl.BlockSpec((PEER_HEADS, keys, tb), lambda t, e: (0, e, t)),
            pl.BlockSpec((PEER_HEADS, PEER_NKEYS, tb), lambda t, e: (0, 0, t)),
            pl.BlockSpec((PEER_HEADS, PEER_NKEYS, tb), lambda t, e: (0, 0, t)),
            pl.BlockSpec((tb, D_MODEL), lambda t, e: (t, 0)),
            pl.BlockSpec((1, D_MODEL), lambda t, e: (0, 0)),
        ],
        out_specs=pl.BlockSpec((tb, D_MODEL), lambda t, e: (t, 0)),
        out_shape=jax.ShapeDtypeStruct((TOKENS, D_MODEL), F32),
        scratch_shapes=[pltpu.VMEM((D_MODEL, tb), F32), pltpu.VMEM((eb, tb), BF16)],
        compiler_params=pltpu.CompilerParams(
            dimension_semantics=("arbitrary", "arbitrary"),
            vmem_limit_bytes=_vmem_limit(blk + scratch // 2)),
        name="peer_dense",
    )(n2, u, vt, p, c, q, r, h, fg)


def kernel(x, norm1_g, w_in, w_gate_up, b_gate, gla_norm_g, sgu_ln_g, sgu_ln_b, sgu_w, sgu_b,
           w_branch_a, w_branch_b, w_out, norm2_g, peer_wq, peer_k1, peer_k2, peer_u, peer_v,
           final_g):
    assert x.shape == (BATCH, SEQ, D_MODEL) and w_in.shape[0] == 1
    row = lambda a: a.reshape(1, -1).astype(F32)

    w = w_in[0]
    o_alr = 4 * GLA_W
    o_su = o_alr + GLA_LOWRANK
    o_sv = o_su + SGU_W
    o_ga = o_sv + SGU_W
    o_gb = o_ga + D_MODEL
    alr_pad = jnp.pad(w[:, o_alr:o_su], ((0, 0), (0, V7X_LANES - GLA_LOWRANK)))
    win = jnp.concatenate([w[:, :o_alr], w[:, o_su:o_sv], w[:, o_sv:o_ga], w[:, o_ga:o_gb],
                           w[:, o_gb:], alr_pad], axis=1).astype(BF16)
    wa2 = jnp.pad(w_gate_up[0], ((0, V7X_LANES - GLA_LOWRANK), (0, 0))).astype(BF16)
    sbb = jnp.broadcast_to(sgu_b[0][:, :, None], (SGU_GROUPS, SGU_BLOCK, SGU_BLOCK)).astype(F32)

    h, n2 = _mixers(x, row(norm1_g[0]), win, wa2, row(b_gate[0]), row(gla_norm_g[0]),
                    row(sgu_ln_g[0]), row(sgu_ln_b[0]), sgu_w[0], sbb,
                    w_branch_a[0].astype(BF16), w_branch_b[0].astype(BF16),
                    w_out[0].astype(BF16), row(norm2_g[0]))
    h = h.reshape(TOKENS, D_MODEL)
    n2 = n2.reshape(TOKENS, D_MODEL)

    wfold = _peer_fold(jnp.stack([peer_k1[0], peer_k2[0]]), peer_wq[0])
    hk = PEER_HEADS * PEER_NKEYS
    p, c, q, r = _peer_select(n2, wfold[:hk], wfold[hk:])

    u = peer_u[0].astype(BF16)
    vt = peer_v[0].T.astype(BF16)
    y = _peer_dense(n2, u, vt, p, c, q, r, h, row(final_g))
    return y.reshape(BATCH, SEQ, D_MODEL)
```

```python
import functools
import math

import jax
import jax.numpy as jnp
import numpy as np
from jax import lax
from jax.experimental import pallas as pl
from jax.experimental.pallas import tpu as pltpu

F32 = jnp.float32
BF16 = jnp.bfloat16

D_MODEL = 1024
BATCH = 8
SEQ = 4096
TOKENS = BATCH * SEQ
CHUNK = 64
EPS = 1e-6
GLA_HEADS = 4
GLA_DK = 128
GLA_DV = 128
GLA_LOWRANK = 16
GLA_TAU = 16.0
GLA_W = GLA_HEADS * GLA_DK
SGU_GROUPS = 4
SGU_BLOCK = 128
SGU_W = 512
PEER_HEADS = 8
PEER_DKEY = 256
PEER_NKEYS = 128
PEER_TOPK = 16
PEER_EXPERTS = PEER_NKEYS * PEER_NKEYS

V7X_LANES = 128
V7X_SUBLANES = 8
V7X_BF16_ROWS = 16
V7X_VMEM_BYTES = 64 * 1024 * 1024

_C_QKVR = 0
_C_SU = 2048
_C_SV = 2560
_C_GA = 3072
_C_GB = 4096
_C_ALR = 5120
_IN_PACKED = 5248

MIX_SB = 256
SEL_TB = 256
DENSE_TB = 512
DENSE_EB = 2048
DENSE_HC = 1024

_NT = (((1,), (1,)), ((), ()))
_TN = (((0,), (0,)), ((), ()))


def _dot(a, b):
    return jnp.dot(a, b, preferred_element_type=F32)


def _dot_nt(a, b):
    return lax.dot_general(a, b, _NT, preferred_element_type=F32)


def _vmem_limit(block_bytes):
    return int(min(V7X_VMEM_BYTES - 8 * 1024 * 1024, 2 * block_bytes + 16 * 1024 * 1024))


def _oddeven_merge_sort_pairs(n):
    pairs = []
    p = 1
    while p < n:
        k = p
        while k >= 1:
            for j in range(k % p, n - k, 2 * k):
                for i in range(min(k, n - j - k)):
                    if (i + j) // (2 * p) == (i + j + k) // (2 * p):
                        pairs.append((i + j, i + j + k))
            k //= 2
        p *= 2
    return pairs


_SORT16 = _oddeven_merge_sort_pairs(16)
_CLEAN16 = [(i, i + d) for d in (8, 4, 2, 1) for i in range(16) if not i & d]


def _cmpx(xs, i, j):
    a, b = xs[i], xs[j]
    if b is None:
        return
    if a is None:
        xs[i], xs[j] = b, None
        return
    xs[i] = jnp.maximum(a, b)
    xs[j] = jnp.minimum(a, b)


def _sort16(xs):
    xs = list(xs)
    for i, j in _SORT16:
        _cmpx(xs, i, j)
    return xs


def _merge_top16(a, b):
    z = []
    for k in range(16):
        x, y = a[k], b[15 - k]
        if y is None:
            z.append(x)
        elif x is None:
            z.append(y)
        else:
            z.append(jnp.maximum(x, y))
    for i, j in _CLEAN16:
        _cmpx(z, i, j)
    return z


def _top16_of(vals):
    vals = list(vals) + [None] * (-len(vals) % 16)
    groups = [_sort16(vals[g:g + 16]) for g in range(0, len(vals), 16)]
    while len(groups) > 1:
        nxt = [_merge_top16(groups[g], groups[g + 1]) for g in range(0, len(groups) - 1, 2)]
        if len(groups) % 2:
            nxt.append(groups[-1])
        groups = nxt
    return groups[0]


def _top16_rows(tiles):
    xs = _sort16(tiles)
    for shift in (4, 2, 1):
        rolled = [pltpu.roll(x, shift, 0) for x in xs]
        xs = _merge_top16(xs, rolled)
    return xs


def _fold_kernel(k_ref, wq_ref, o_ref):
    o_ref[...] = lax.dot_general(k_ref[0], wq_ref[...], _NT,
                                 precision=lax.Precision.HIGHEST,
                                 preferred_element_type=F32).astype(o_ref.dtype)


def _peer_fold(k12, wq):
    half = PEER_DKEY // 2
    return pl.pallas_call(
        _fold_kernel,
        grid=(2, PEER_HEADS),
        in_specs=[
            pl.BlockSpec((1, PEER_NKEYS, half), lambda s, h: (s, 0, 0)),
            pl.BlockSpec((D_MODEL, half), lambda s, h: (0, 2 * h + s)),
        ],
        out_specs=pl.BlockSpec((PEER_NKEYS, D_MODEL), lambda s, h: (s * PEER_HEADS + h, 0)),
        out_shape=jax.ShapeDtypeStruct((2 * PEER_HEADS * PEER_NKEYS, D_MODEL), BF16),
        name="peer_fold",
    )(k12, wq)


def _mix_kernel(x_ref, g1_ref, win_ref, wa2_ref, bg_ref, gno_ref, lng_ref, lnb_ref,
                sw_ref, sbb_ref, wba_ref, wbb_ref, wout_ref, g2_ref,
                h_ref, n2_ref, st_ref, ya_ref, yb_ref, later_ref, wm_ref):
    sb = MIX_SB
    chunk_shift = CHUNK.bit_length() - 1

    @pl.when((pl.program_id(0) == 0) & (pl.program_id(1) == 0))
    def _():
        ri = lax.broadcasted_iota(jnp.int32, (sb, sb), 0)
        ci = lax.broadcasted_iota(jnp.int32, (sb, sb), 1)
        same = (ci >> chunk_shift) == (ri >> chunk_shift)
        later_ref[...] = jnp.where((ci > ri) & same, 1.0, 0.0).astype(BF16)
        bi = lax.broadcasted_iota(jnp.int32, (SGU_BLOCK, SGU_BLOCK), 0)
        bj = lax.broadcasted_iota(jnp.int32, (SGU_BLOCK, SGU_BLOCK), 1)
        causal = (bi >> chunk_shift) >= (bj >> chunk_shift)
        for g in range(SGU_GROUPS):
            wm_ref[g] = jnp.where(causal, sw_ref[g], 0.0).astype(BF16)

    @pl.when(pl.program_id(1) == 0)
    def _():
        st_ref[...] = jnp.zeros_like(st_ref)

    x = x_ref[0]
    n1 = x * lax.rsqrt(jnp.mean(x * x, axis=-1, keepdims=True) + EPS) * g1_ref[...]
    n1b = n1.astype(BF16)

    qkvr = _dot(n1b, win_ref[:, _C_QKVR:_C_QKVR + 4 * GLA_W])
    alr = _dot(n1b, win_ref[:, _C_ALR:_C_ALR + V7X_LANES])
    z = _dot(alr.astype(BF16), wa2_ref[...]) + bg_ref[...]
    su_pre = _dot(n1b, win_ref[:, _C_SU:_C_SU + SGU_W])
    sv_pre = _dot(n1b, win_ref[:, _C_SV:_C_SV + SGU_W])

    q = qkvr[:, 0:GLA_W] * (GLA_DK ** -0.5)
    k = qkvr[:, GLA_W:2 * GLA_W]
    v = qkvr[:, 2 * GLA_W:3 * GLA_W]
    r = qkvr[:, 3 * GLA_W:4 * GLA_W]
    la = (jnp.minimum(z, 0.0) - jnp.log1p(jnp.exp(-jnp.abs(z)))) * (1.0 / GLA_TAU)
    la_hi = la.astype(BF16)
    la_lo = (la - la_hi.astype(F32)).astype(BF16)
    dec = _dot(later_ref[...], la_hi) + _dot(later_ref[...], la_lo)
    ga_pre = _dot(n1b, win_ref[:, _C_GA:_C_GA + D_MODEL])
    gb_pre = _dot(n1b, win_ref[:, _C_GB:_C_GB + D_MODEL])

    kdec = (k * jnp.exp(dec)).astype(BF16)
    vb = v.astype(BF16)
    qb = q.astype(BF16)
    heads = [slice(hh * GLA_DK, (hh + 1) * GLA_DK) for hh in range(GLA_HEADS)]
    chunks = [slice(c * CHUNK, (c + 1) * CHUNK) for c in range(sb // CHUNK)]
    kv_t = [[lax.dot_general(vb[rs, ls], kdec[rs, ls], _TN, preferred_element_type=F32)
             for ls in heads] for rs in chunks]

    su = jax.nn.gelu(su_pre)
    sv = jax.nn.gelu(sv_pre)
    mu = jnp.mean(sv, axis=-1, keepdims=True)
    svc = sv - mu
    var = jnp.mean(svc * svc, axis=-1, keepdims=True)
    vn = (svc * lax.rsqrt(var + EPS) * lng_ref[...] + lnb_ref[...]).astype(BF16)
    for g in range(SGU_GROUPS):
        ls = slice(g * SGU_BLOCK, (g + 1) * SGU_BLOCK)
        for nb in range(sb // SGU_BLOCK):
            rs = slice(nb * SGU_BLOCK, (nb + 1) * SGU_BLOCK)
            mixed = _dot(wm_ref[g], vn[rs, ls]) + sbb_ref[g]
            yb_ref[rs, ls] = su[rs, ls] * mixed
    y_b = yb_ref[...]

    for hh, ls in enumerate(heads):
        st = st_ref[hh]
        for c, rs in enumerate(chunks):
            tot = dec[c * CHUNK:c * CHUNK + 1, ls] + la[c * CHUNK:c * CHUNK + 1, ls]
            st = st * jnp.exp(tot) + kv_t[c][hh]
            o = _dot_nt(qb[rs, ls], st.astype(BF16))
            o = o * lax.rsqrt(jnp.mean(o * o, axis=-1, keepdims=True) + EPS)
            ya_ref[rs, ls] = o
        st_ref[hh] = st
    y_a = ya_ref[...] * gno_ref[...] * (r * jax.nn.sigmoid(r))

    ga = jax.nn.sigmoid(ga_pre)
    gb = jax.nn.sigmoid(gb_pre)
    merged = ga * _dot(y_a.astype(BF16), wba_ref[...]) + gb * _dot(y_b.astype(BF16), wbb_ref[...])
    h = x + _dot(merged.astype(BF16), wout_ref[...])
    h_ref[0] = h
    n2 = h * lax.rsqrt(jnp.mean(h * h, axis=-1, keepdims=True) + EPS) * g2_ref[...]
    n2_ref[...] = n2.T.astype(BF16)


def _mixers(x, g1, win, wa2, bg, gno, lng, lnb, sw, sbb, wba, wbb, wout, g2):
    sb = MIX_SB
    full = lambda a: pl.BlockSpec(a.shape, lambda b, s, _n=a.ndim: (0,) * _n)
    consts = (g1, win, wa2, bg, gno, lng, lnb, sw, sbb, wba, wbb, wout, g2)
    blk = sum(int(np.prod(a.shape)) * a.dtype.itemsize for a in consts)
    blk += sb * D_MODEL * (4 + 4 + 2)
    return pl.pallas_call(
        _mix_kernel,
        grid=(BATCH, SEQ // sb),
        in_specs=[pl.BlockSpec((1, sb, D_MODEL), lambda b, s: (b, s, 0))] + [full(a) for a in consts],
        out_specs=[pl.BlockSpec((1, sb, D_MODEL), lambda b, s: (b, s, 0)),
                   pl.BlockSpec((D_MODEL, sb), lambda b, s: (0, b * (SEQ // sb) + s))],
        out_shape=[jax.ShapeDtypeStruct((BATCH, SEQ, D_MODEL), F32),
                   jax.ShapeDtypeStruct((D_MODEL, TOKENS), BF16)],
        scratch_shapes=[pltpu.VMEM((GLA_HEADS, GLA_DV, GLA_DK), F32),
                        pltpu.VMEM((sb, GLA_W), F32),
                        pltpu.VMEM((sb, SGU_W), F32),
                        pltpu.VMEM((sb, sb), BF16),
                        pltpu.VMEM((SGU_GROUPS, SGU_BLOCK, SGU_BLOCK), BF16)],
        compiler_params=pltpu.CompilerParams(
            dimension_semantics=("arbitrary", "arbitrary"),
            vmem_limit_bytes=_vmem_limit(blk)),
        name="mixers",
    )(x, *consts)


def _row(ref, idx, h, rows):
    return jnp.broadcast_to(ref[idx, pl.ds(h, 1), :], (rows, ref.shape[-1]))


def _select_kernel(n2_ref, w1_ref, w2_ref, p_ref, c_ref, q_ref, r_ref,
                   s1_ref, s2_ref, v1_ref, v2_ref, sg_ref, aux_ref):
    tb = SEL_TB
    n2t = n2_ref[...]
    s1_ref[...] = _dot(w1_ref[...], n2t)
    s2_ref[...] = _dot(w2_ref[...], n2t)

    def top_body(h, carry):
        base = pl.multiple_of(h * PEER_NKEYS, PEER_NKEYS)
        for s_ref, v_ref in ((s1_ref, v1_ref), (s2_ref, v2_ref)):
            tiles = [s_ref[pl.ds(base + V7X_SUBLANES * t, V7X_SUBLANES), :]
                     for t in range(PEER_NKEYS // V7X_SUBLANES)]
            top = _top16_rows(tiles)
            for i in range(PEER_TOPK):
                v_ref[i, pl.ds(h, 1), :] = top[i][0:1, :]
        return carry

    lax.fori_loop(0, PEER_HEADS, top_body, 0)

    v1 = [v1_ref[i] for i in range(PEER_TOPK)]
    v2 = [v2_ref[i] for i in range(PEER_TOPK)]
    cand = {(i, j): v1[i] + v2[j] for i in range(PEER_TOPK) for j in range(PEER_TOPK)
            if (i + 1) * (j + 1) <= PEER_TOPK}
    thr = _top16_of(list(cand.values()))[PEER_TOPK - 1]
    cmax = cand[(0, 0)]
    zsum = jnp.zeros((PEER_HEADS, tb), F32)
    for cnd in cand.values():
        zsum = zsum + jnp.where(cnd >= thr, jnp.exp(cnd - cmax), 0.0)
    aux_ref[0] = 1.0 / zsum
    for j in range(PEER_TOPK):
        sig = jnp.full((PEER_HEADS, tb), jnp.inf, F32)
        for i in range(PEER_TOPK):
            if (i, j) in cand:
                sig = jnp.where(cand[(i, j)] >= thr, v1[i], sig)
        sg_ref[j] = sig

    def emit_body(h, carry):
        base = pl.multiple_of(h * PEER_NKEYS, PEER_NKEYS)
        v2b = [_row(v2_ref, j, h, V7X_BF16_ROWS) for j in range(PEER_TOPK)]
        sgb = [_row(sg_ref, j, h, V7X_BF16_ROWS) for j in range(PEER_TOPK)]
        iz_b = _row(aux_ref, 0, h, V7X_BF16_ROWS)
        m1_b = _row(v1_ref, 0, h, V7X_BF16_ROWS)
        m2_b = v2b[0]
        for t in range(PEER_NKEYS // V7X_BF16_ROWS):
            rs = pl.ds(base + V7X_BF16_ROWS * t, V7X_BF16_ROWS)
            os = pl.ds(V7X_BF16_ROWS * t, V7X_BF16_ROWS)
            s1t = s1_ref[rs, :]
            s2t = s2_ref[rs, :]
            cnt = jnp.zeros_like(s1t)
            rank = jnp.zeros_like(s2t)
            for j in range(PEER_TOPK):
                cnt = jnp.where(s1t >= sgb[j], float(j + 1), cnt)
                rank = jnp.where(v2b[j] > s2t, float(j + 1), rank)
            c_ref[h, os, :] = cnt
            p_ref[h, os, :] = jnp.exp(s1t - m1_b)
            r_ref[h, os, :] = rank.astype(BF16)
            q_ref[h, os, :] = (jnp.exp(s2t - m2_b) * iz_b).astype(BF16)
        return carry

    lax.fori_loop(0, PEER_HEADS, emit_body, 0)


def _peer_select(n2, w1, w2):
    tb = SEL_TB
    hk = PEER_HEADS * PEER_NKEYS
    sel = lambda dt: jax.ShapeDtypeStruct((PEER_HEADS, PEER_NKEYS, TOKENS), dt)
    ospec = pl.BlockSpec((PEER_HEADS, PEER_NKEYS, tb), lambda i: (0, 0, i))
    blk = tb * D_MODEL * 2 + 2 * hk * D_MODEL * 2 + hk * tb * (4 + 4 + 2 + 2)
    return pl.pallas_call(
        _select_kernel,
        grid=(TOKENS // tb,),
        in_specs=[pl.BlockSpec((D_MODEL, tb), lambda i: (0, i)),
                  pl.BlockSpec((hk, D_MODEL), lambda i: (0, 0)),
                  pl.BlockSpec((hk, D_MODEL), lambda i: (0, 0))],
        out_specs=[ospec, ospec, ospec, ospec],
        out_shape=[sel(F32), sel(F32), sel(BF16), sel(BF16)],
        scratch_shapes=[pltpu.VMEM((hk, tb), F32), pltpu.VMEM((hk, tb), F32),
                        pltpu.VMEM((PEER_TOPK, PEER_HEADS, tb), F32),
                        pltpu.VMEM((PEER_TOPK, PEER_HEADS, tb), F32),
                        pltpu.VMEM((PEER_TOPK, PEER_HEADS, tb), F32),
                        pltpu.VMEM((1, PEER_HEADS, tb), F32)],
        compiler_params=pltpu.CompilerParams(
            dimension_semantics=("arbitrary",),
            vmem_limit_bytes=_vmem_limit(blk)),
        name="peer_select",
    )(n2, w1, w2)


def _dense_kernel(n2_ref, u_ref, vt_ref, p_ref, c_ref, q_ref, r_ref, h_ref, fg_ref,
                  y_ref, acc_ref, at_ref):
    tb = DENSE_TB
    eb = pl.program_id(1)

    @pl.when(eb == 0)
    def _():
        acc_ref[...] = jnp.zeros_like(acc_ref)

    n2t = n2_ref[...]
    keys_per_chunk = DENSE_HC // PEER_NKEYS
    for hc in range(DENSE_EB // DENSE_HC):
        ht = _dot(u_ref[hc * DENSE_HC:(hc + 1) * DENSE_HC, :], n2t)
        for al in range(keys_per_chunk):
            a = hc * keys_per_chunk + al
            act = jax.nn.gelu(ht[al * PEER_NKEYS:(al + 1) * PEER_NKEYS, :].astype(BF16))
            gate = None
            for h in range(PEER_HEADS):
                prow = jnp.broadcast_to(p_ref[h, a:a + 1, :], (V7X_BF16_ROWS, tb)).astype(BF16)
                crow = jnp.broadcast_to(c_ref[h, a:a + 1, :], (V7X_BF16_ROWS, tb)).astype(BF16)
                prow = pltpu.repeat(prow, PEER_NKEYS // V7X_BF16_ROWS, 0)
                crow = pltpu.repeat(crow, PEER_NKEYS // V7X_BF16_ROWS, 0)
                term = jnp.where(r_ref[h] < crow, q_ref[h] * prow, jnp.zeros((), BF16))
                gate = term if gate is None else gate + term
            at_ref[a * PEER_NKEYS:(a + 1) * PEER_NKEYS, :] = act * gate
    acc_ref[...] += _dot(vt_ref[...], at_ref[...])

    @pl.when(eb == pl.num_programs(1) - 1)
    def _():
        h2 = h_ref[...] + acc_ref[...].T
        y = h2 * lax.rsqrt(jnp.mean(h2 * h2, axis=-1, keepdims=True) + EPS) * fg_ref[...]
        y_ref[...] = y


def _peer_dense(n2, u, vt, p, c, q, r, h, fg):
    tb, eb = DENSE_TB, DENSE_EB
    keys = eb // PEER_NKEYS
    blk = (tb * D_MODEL * 2 + 2 * eb * D_MODEL * 2 + 2 * PEER_HEADS * keys * tb * 4
           + 2 * PEER_HEADS * PEER_NKEYS * tb * 2 + 2 * tb * D_MODEL * 4)
    scratch = D_MODEL * tb * 4 + eb * tb * 2
    return pl.pallas_call(
        _dense_kernel,
        grid=(TOKENS // tb, PEER_EXPERTS // eb),
        in_specs=[
            pl.BlockSpec((D_MODEL, tb), lambda t, e: (0, t)),
            pl.BlockSpec((eb, D_MODEL), lambda t, e: (e, 0)),
            pl.BlockSpec((D_MODEL, eb), lambda t, e: (0, e)),
            pl.BlockSpec((PEER_HEADS, keys, tb), lambda t, e: (0, e, t)),
            pl.BlockSpec((PEER_HEADS, keys, tb), lambda t, e: (0, e, t)),
            pl.BlockSpec((PEER_HEADS, PEER_NKEYS, tb), lambda t, e: (0, 0, t)),
            pl.BlockSpec((PEER_HEADS, PEER_NKEYS, tb), lambda t, e: (0, 0, t)),
            pl.BlockSpec((tb, D_MODEL), lambda t, e: (t, 0)),
            pl.BlockSpec((1, D_MODEL), lambda t, e: (0, 0)),
        ],
        out_specs=pl.BlockSpec((tb, D_MODEL), lambda t, e: (t, 0)),
        out_shape=jax.ShapeDtypeStruct((TOKENS, D_MODEL), F32),
        scratch_shapes=[pltpu.VMEM((D_MODEL, tb), F32), pltpu.VMEM((eb, tb), BF16)],
        compiler_params=pltpu.CompilerParams(
            dimension_semantics=("arbitrary", "arbitrary"),
            vmem_limit_bytes=_vmem_limit(blk + scratch // 2)),
        name="peer_dense",
    )(n2, u, vt, p, c, q, r, h, fg)


def kernel(x, norm1_g, w_in, w_gate_up, b_gate, gla_norm_g, sgu_ln_g, sgu_ln_b, sgu_w, sgu_b,
           w_branch_a, w_branch_b, w_out, norm2_g, peer_wq, peer_k1, peer_k2, peer_u, peer_v,
           final_g):
    assert x.shape == (BATCH, SEQ, D_MODEL) and w_in.shape[0] == 1
    row = lambda a: a.reshape(1, -1).astype(F32)

    w = w_in[0]
    o_alr = 4 * GLA_W
    o_su = o_alr + GLA_LOWRANK
    o_sv = o_su + SGU_W
    o_ga = o_sv + SGU_W
    o_gb = o_ga + D_MODEL
    alr_pad = jnp.pad(w[:, o_alr:o_su], ((0, 0), (0, V7X_LANES - GLA_LOWRANK)))
    win = jnp.concatenate([w[:, :o_alr], w[:, o_su:o_sv], w[:, o_sv:o_ga], w[:, o_ga:o_gb],
                           w[:, o_gb:], alr_pad], axis=1).astype(BF16)
    wa2 = jnp.pad(w_gate_up[0], ((0, V7X_LANES - GLA_LOWRANK), (0, 0))).astype(BF16)
    sbb = jnp.broadcast_to(sgu_b[0][:, :, None], (SGU_GROUPS, SGU_BLOCK, SGU_BLOCK)).astype(F32)

    h, n2 = _mixers(x, row(norm1_g[0]), win, wa2, row(b_gate[0]), row(gla_norm_g[0]),
                    row(sgu_ln_g[0]), row(sgu_ln_b[0]), sgu_w[0], sbb,
                    w_branch_a[0].astype(BF16), w_branch_b[0].astype(BF16),
                    w_out[0].astype(BF16), row(norm2_g[0]))
    h = h.reshape(TOKENS, D_MODEL)

    wfold = _peer_fold(jnp.stack([peer_k1[0], peer_k2[0]]), peer_wq[0])
    hk = PEER_HEADS * PEER_NKEYS
    p, c, q, r = _peer_select(n2, wfold[:hk], wfold[hk:])

    u = peer_u[0].astype(BF16)
    vt = peer_v[0].T.astype(BF16)
    y = _peer_dense(n2, u, vt, p, c, q, r, h, row(final_g))
    return y.reshape(BATCH, SEQ, D_MODEL)
```

```python
import functools
import math

import jax
import jax.numpy as jnp
import numpy as np
from jax import lax
from jax.experimental import pallas as pl
from jax.experimental.pallas import tpu as pltpu

F32 = jnp.float32
BF16 = jnp.bfloat16
FP8 = jnp.float8_e4m3fn
FP8_MAX = 448.0

D_MODEL = 1024
BATCH = 8
SEQ = 4096
TOKENS = BATCH * SEQ
CHUNK = 64
EPS = 1e-6
GLA_HEADS = 4
GLA_DK = 128
GLA_DV = 128
GLA_LOWRANK = 16
GLA_TAU = 16.0
GLA_W = GLA_HEADS * GLA_DK
SGU_GROUPS = 4
SGU_BLOCK = 128
SGU_W = 512
PEER_HEADS = 8
PEER_DKEY = 256
PEER_NKEYS = 128
PEER_TOPK = 16
PEER_EXPERTS = PEER_NKEYS * PEER_NKEYS

V7X_LANES = 128
V7X_SUBLANES = 8
V7X_BF16_ROWS = 16
V7X_VMEM_BYTES = 64 * 1024 * 1024

_C_QKVR = 0
_C_SU = 2048
_C_SV = 2560
_C_GA = 3072
_C_GB = 4096
_C_ALR = 5120
_IN_PACKED = 5248

MIX_SB = 256
SEL_TB = 256
DENSE_TB = 512
DENSE_EB = 2048
DENSE_HC = 1024

_NT = (((1,), (1,)), ((), ()))
_TN = (((0,), (0,)), ((), ()))


def _dot(a, b):
    return jnp.dot(a, b, preferred_element_type=F32)


def _dot_nt(a, b):
    return lax.dot_general(a, b, _NT, preferred_element_type=F32)


def _pow2_scale(bound):
    return jnp.exp2(jnp.floor(jnp.log2(0.5 * FP8_MAX / jnp.maximum(bound, 1e-30))))


def _vmem_limit(block_bytes):
    return int(min(V7X_VMEM_BYTES - 8 * 1024 * 1024, 2 * block_bytes + 16 * 1024 * 1024))


def _oddeven_merge_sort_pairs(n):
    pairs = []
    p = 1
    while p < n:
        k = p
        while k >= 1:
            for j in range(k % p, n - k, 2 * k):
                for i in range(min(k, n - j - k)):
                    if (i + j) // (2 * p) == (i + j + k) // (2 * p):
                        pairs.append((i + j, i + j + k))
            k //= 2
        p *= 2
    return pairs


_SORT16 = _oddeven_merge_sort_pairs(16)
_CLEAN16 = [(i, i + d) for d in (8, 4, 2, 1) for i in range(16) if not i & d]


def _cmpx(xs, i, j):
    a, b = xs[i], xs[j]
    if b is None:
        return
    if a is None:
        xs[i], xs[j] = b, None
        return
    xs[i] = jnp.maximum(a, b)
    xs[j] = jnp.minimum(a, b)


def _sort16(xs):
    xs = list(xs)
    for i, j in _SORT16:
        _cmpx(xs, i, j)
    return xs


def _merge_top16(a, b):
    z = []
    for k in range(16):
        x, y = a[k], b[15 - k]
        if y is None:
            z.append(x)
        elif x is None:
            z.append(y)
        else:
            z.append(jnp.maximum(x, y))
    for i, j in _CLEAN16:
        _cmpx(z, i, j)
    return z


def _top16_of(vals):
    vals = list(vals) + [None] * (-len(vals) % 16)
    groups = [_sort16(vals[g:g + 16]) for g in range(0, len(vals), 16)]
    while len(groups) > 1:
        nxt = [_merge_top16(groups[g], groups[g + 1]) for g in range(0, len(groups) - 1, 2)]
        if len(groups) % 2:
            nxt.append(groups[-1])
        groups = nxt
    return groups[0]


def _top16_rows(tiles):
    xs = _sort16(tiles)
    for shift in (4, 2, 1):
        rolled = [pltpu.roll(x, shift, 0) for x in xs]
        xs = _merge_top16(xs, rolled)
    return xs


def _fold_kernel(k_ref, wq_ref, o_ref):
    o_ref[...] = lax.dot_general(k_ref[0], wq_ref[...], _NT,
                                 precision=lax.Precision.HIGHEST,
                                 preferred_element_type=F32).astype(o_ref.dtype)


def _peer_fold(k12, wq):
    half = PEER_DKEY // 2
    return pl.pallas_call(
        _fold_kernel,
        grid=(2, PEER_HEADS),
        in_specs=[
            pl.BlockSpec((1, PEER_NKEYS, half), lambda s, h: (s, 0, 0)),
            pl.BlockSpec((D_MODEL, half), lambda s, h: (0, 2 * h + s)),
        ],
        out_specs=pl.BlockSpec((PEER_NKEYS, D_MODEL), lambda s, h: (s * PEER_HEADS + h, 0)),
        out_shape=jax.ShapeDtypeStruct((2 * PEER_HEADS * PEER_NKEYS, D_MODEL), BF16),
        name="peer_fold",
    )(k12, wq)


def _mix_kernel(sn_ref, x_ref, g1_ref, win_ref, wa2_ref, bg_ref, gno_ref, lng_ref, lnb_ref,
                sw_ref, sbb_ref, wba_ref, wbb_ref, wout_ref, g2_ref,
                h_ref, n2_ref, n2q_ref, st_ref, ya_ref, yb_ref, later_ref, wm_ref):
    sb = MIX_SB
    chunk_shift = CHUNK.bit_length() - 1

    @pl.when((pl.program_id(0) == 0) & (pl.program_id(1) == 0))
    def _():
        ri = lax.broadcasted_iota(jnp.int32, (sb, sb), 0)
        ci = lax.broadcasted_iota(jnp.int32, (sb, sb), 1)
        same = (ci >> chunk_shift) == (ri >> chunk_shift)
        later_ref[...] = jnp.where((ci > ri) & same, 1.0, 0.0).astype(BF16)
        bi = lax.broadcasted_iota(jnp.int32, (SGU_BLOCK, SGU_BLOCK), 0)
        bj = lax.broadcasted_iota(jnp.int32, (SGU_BLOCK, SGU_BLOCK), 1)
        causal = (bi >> chunk_shift) >= (bj >> chunk_shift)
        for g in range(SGU_GROUPS):
            wm_ref[g] = jnp.where(causal, sw_ref[g], 0.0).astype(BF16)

    @pl.when(pl.program_id(1) == 0)
    def _():
        st_ref[...] = jnp.zeros_like(st_ref)

    x = x_ref[0]
    n1 = x * lax.rsqrt(jnp.mean(x * x, axis=-1, keepdims=True) + EPS) * g1_ref[...]
    n1b = n1.astype(BF16)

    qkvr = _dot(n1b, win_ref[:, _C_QKVR:_C_QKVR + 4 * GLA_W])
    alr = _dot(n1b, win_ref[:, _C_ALR:_C_ALR + V7X_LANES])
    z = _dot(alr.astype(BF16), wa2_ref[...]) + bg_ref[...]
    su_pre = _dot(n1b, win_ref[:, _C_SU:_C_SU + SGU_W])
    sv_pre = _dot(n1b, win_ref[:, _C_SV:_C_SV + SGU_W])

    q = qkvr[:, 0:GLA_W] * (GLA_DK ** -0.5)
    k = qkvr[:, GLA_W:2 * GLA_W]
    v = qkvr[:, 2 * GLA_W:3 * GLA_W]
    r = qkvr[:, 3 * GLA_W:4 * GLA_W]
    la = (jnp.minimum(z, 0.0) - jnp.log1p(jnp.exp(-jnp.abs(z)))) * (1.0 / GLA_TAU)
    la_hi = la.astype(BF16)
    la_lo = (la - la_hi.astype(F32)).astype(BF16)
    dec = _dot(later_ref[...], la_hi) + _dot(later_ref[...], la_lo)
    ga_pre = _dot(n1b, win_ref[:, _C_GA:_C_GA + D_MODEL])
    gb_pre = _dot(n1b, win_ref[:, _C_GB:_C_GB + D_MODEL])

    kdec = (k * jnp.exp(dec)).astype(BF16)
    vb = v.astype(BF16)
    qb = q.astype(BF16)
    heads = [slice(hh * GLA_DK, (hh + 1) * GLA_DK) for hh in range(GLA_HEADS)]
    chunks = [slice(c * CHUNK, (c + 1) * CHUNK) for c in range(sb // CHUNK)]
    kv_t = [[lax.dot_general(vb[rs, ls], kdec[rs, ls], _TN, preferred_element_type=F32)
             for ls in heads] for rs in chunks]

    su = jax.nn.gelu(su_pre)
    sv = jax.nn.gelu(sv_pre)
    mu = jnp.mean(sv, axis=-1, keepdims=True)
    svc = sv - mu
    var = jnp.mean(svc * svc, axis=-1, keepdims=True)
    vn = (svc * lax.rsqrt(var + EPS) * lng_ref[...] + lnb_ref[...]).astype(BF16)
    for g in range(SGU_GROUPS):
        ls = slice(g * SGU_BLOCK, (g + 1) * SGU_BLOCK)
        for nb in range(sb // SGU_BLOCK):
            rs = slice(nb * SGU_BLOCK, (nb + 1) * SGU_BLOCK)
            mixed = _dot(wm_ref[g], vn[rs, ls]) + sbb_ref[g]
            yb_ref[rs, ls] = su[rs, ls] * mixed
    y_b = yb_ref[...]

    for hh, ls in enumerate(heads):
        st = st_ref[hh]
        for c, rs in enumerate(chunks):
            tot = dec[c * CHUNK:c * CHUNK + 1, ls] + la[c * CHUNK:c * CHUNK + 1, ls]
            st = st * jnp.exp(tot) + kv_t[c][hh]
            o = _dot_nt(qb[rs, ls], st.astype(BF16))
            o = o * lax.rsqrt(jnp.mean(o * o, axis=-1, keepdims=True) + EPS)
            ya_ref[rs, ls] = o
        st_ref[hh] = st
    y_a = ya_ref[...] * gno_ref[...] * (r * jax.nn.sigmoid(r))

    ga = jax.nn.sigmoid(ga_pre)
    gb = jax.nn.sigmoid(gb_pre)
    merged = ga * _dot(y_a.astype(BF16), wba_ref[...]) + gb * _dot(y_b.astype(BF16), wbb_ref[...])
    h = x + _dot(merged.astype(BF16), wout_ref[...])
    h_ref[0] = h
    n2 = h * lax.rsqrt(jnp.mean(h * h, axis=-1, keepdims=True) + EPS) * g2_ref[...]
    n2t = n2.T
    n2_ref[...] = n2t.astype(BF16)
    n2q_ref[...] = (n2t * sn_ref[0]).astype(FP8)


def _mixers(sn, x, g1, win, wa2, bg, gno, lng, lnb, sw, sbb, wba, wbb, wout, g2):
    sb = MIX_SB
    full = lambda a: pl.BlockSpec(a.shape, lambda b, s, _n=a.ndim: (0,) * _n)
    consts = (g1, win, wa2, bg, gno, lng, lnb, sw, sbb, wba, wbb, wout, g2)
    blk = sum(int(np.prod(a.shape)) * a.dtype.itemsize for a in consts)
    blk += sb * D_MODEL * (4 + 4 + 2 + 1)
    tspec = pl.BlockSpec((D_MODEL, sb), lambda b, s: (0, b * (SEQ // sb) + s))
    return pl.pallas_call(
        _mix_kernel,
        grid=(BATCH, SEQ // sb),
        in_specs=[pl.BlockSpec(memory_space=pltpu.SMEM),
                  pl.BlockSpec((1, sb, D_MODEL), lambda b, s: (b, s, 0))] + [full(a) for a in consts],
        out_specs=[pl.BlockSpec((1, sb, D_MODEL), lambda b, s: (b, s, 0)), tspec, tspec],
        out_shape=[jax.ShapeDtypeStruct((BATCH, SEQ, D_MODEL), F32),
                   jax.ShapeDtypeStruct((D_MODEL, TOKENS), BF16),
                   jax.ShapeDtypeStruct((D_MODEL, TOKENS), FP8)],
        scratch_shapes=[pltpu.VMEM((GLA_HEADS, GLA_DV, GLA_DK), F32),
                        pltpu.VMEM((sb, GLA_W), F32),
                        pltpu.VMEM((sb, SGU_W), F32),
                        pltpu.VMEM((sb, sb), BF16),
                        pltpu.VMEM((SGU_GROUPS, SGU_BLOCK, SGU_BLOCK), BF16)],
        compiler_params=pltpu.CompilerParams(
            dimension_semantics=("arbitrary", "arbitrary"),
            vmem_limit_bytes=_vmem_limit(blk)),
        name="mixers",
    )(sn, x, *consts)


def _row(ref, idx, h, rows):
    return jnp.broadcast_to(ref[idx, pl.ds(h, 1), :], (rows, ref.shape[-1]))


def _select_kernel(n2_ref, w1_ref, w2_ref, p_ref, c_ref, q_ref, r_ref,
                   s1_ref, s2_ref, v1_ref, v2_ref, sg_ref, aux_ref):
    tb = SEL_TB
    n2t = n2_ref[...]
    s1_ref[...] = _dot(w1_ref[...], n2t)
    s2_ref[...] = _dot(w2_ref[...], n2t)

    def top_body(h, carry):
        base = pl.multiple_of(h * PEER_NKEYS, PEER_NKEYS)
        for s_ref, v_ref in ((s1_ref, v1_ref), (s2_ref, v2_ref)):
            tiles = [s_ref[pl.ds(base + V7X_SUBLANES * t, V7X_SUBLANES), :]
                     for t in range(PEER_NKEYS // V7X_SUBLANES)]
            top = _top16_rows(tiles)
            for i in range(PEER_TOPK):
                v_ref[i, pl.ds(h, 1), :] = top[i][0:1, :]
        return carry

    lax.fori_loop(0, PEER_HEADS, top_body, 0)

    v1 = [v1_ref[i] for i in range(PEER_TOPK)]
    v2 = [v2_ref[i] for i in range(PEER_TOPK)]
    cand = {(i, j): v1[i] + v2[j] for i in range(PEER_TOPK) for j in range(PEER_TOPK)
            if (i + 1) * (j + 1) <= PEER_TOPK}
    thr = _top16_of(list(cand.values()))[PEER_TOPK - 1]
    cmax = cand[(0, 0)]
    zsum = jnp.zeros((PEER_HEADS, tb), F32)
    for cnd in cand.values():
        zsum = zsum + jnp.where(cnd >= thr, jnp.exp(cnd - cmax), 0.0)
    aux_ref[0] = 1.0 / zsum
    for j in range(PEER_TOPK):
        sig = jnp.full((PEER_HEADS, tb), jnp.inf, F32)
        for i in range(PEER_TOPK):
            if (i, j) in cand:
                sig = jnp.where(cand[(i, j)] >= thr, v1[i], sig)
        sg_ref[j] = sig

    def emit_body(h, carry):
        base = pl.multiple_of(h * PEER_NKEYS, PEER_NKEYS)
        v2b = [_row(v2_ref, j, h, V7X_BF16_ROWS) for j in range(PEER_TOPK)]
        sgb = [_row(sg_ref, j, h, V7X_BF16_ROWS) for j in range(PEER_TOPK)]
        iz_b = _row(aux_ref, 0, h, V7X_BF16_ROWS)
        m1_b = _row(v1_ref, 0, h, V7X_BF16_ROWS)
        m2_b = v2b[0]
        for t in range(PEER_NKEYS // V7X_BF16_ROWS):
            rs = pl.ds(base + V7X_BF16_ROWS * t, V7X_BF16_ROWS)
            os = pl.ds(V7X_BF16_ROWS * t, V7X_BF16_ROWS)
            s1t = s1_ref[rs, :]
            s2t = s2_ref[rs, :]
            cnt = jnp.zeros_like(s1t)
            rank = jnp.zeros_like(s2t)
            for j in range(PEER_TOPK):
                cnt = jnp.where(s1t >= sgb[j], float(j + 1), cnt)
                rank = jnp.where(v2b[j] > s2t, float(j + 1), rank)
            pexp = jnp.exp(s1t - m1_b)
            for lt in range(tb // V7X_LANES):
                ln = slice(lt * V7X_LANES, (lt + 1) * V7X_LANES)
                c_ref[h, lt, os, :] = cnt[:, ln]
                p_ref[h, lt, os, :] = pexp[:, ln]
            r_ref[h, os, :] = rank.astype(BF16)
            q_ref[h, os, :] = (jnp.exp(s2t - m2_b) * iz_b).astype(BF16)
        return carry

    lax.fori_loop(0, PEER_HEADS, emit_body, 0)


def _peer_select(n2, w1, w2):
    tb = SEL_TB
    hk = PEER_HEADS * PEER_NKEYS
    sel = lambda dt: jax.ShapeDtypeStruct((PEER_HEADS, PEER_NKEYS, TOKENS), dt)
    ospec = pl.BlockSpec((PEER_HEADS, PEER_NKEYS, tb), lambda i: (0, 0, i))
    rows = jax.ShapeDtypeStruct((PEER_HEADS, TOKENS // V7X_LANES, PEER_NKEYS, V7X_LANES), F32)
    rspec = pl.BlockSpec((PEER_HEADS, tb // V7X_LANES, PEER_NKEYS, V7X_LANES),
                         lambda i: (0, i, 0, 0))
    blk = tb * D_MODEL * 2 + 2 * hk * D_MODEL * 2 + hk * tb * (4 + 4 + 2 + 2)
    return pl.pallas_call(
        _select_kernel,
        grid=(TOKENS // tb,),
        in_specs=[pl.BlockSpec((D_MODEL, tb), lambda i: (0, i)),
                  pl.BlockSpec((hk, D_MODEL), lambda i: (0, 0)),
                  pl.BlockSpec((hk, D_MODEL), lambda i: (0, 0))],
        out_specs=[rspec, rspec, ospec, ospec],
        out_shape=[rows, rows, sel(BF16), sel(BF16)],
        scratch_shapes=[pltpu.VMEM((hk, tb), F32), pltpu.VMEM((hk, tb), F32),
                        pltpu.VMEM((PEER_TOPK, PEER_HEADS, tb), F32),
                        pltpu.VMEM((PEER_TOPK, PEER_HEADS, tb), F32),
                        pltpu.VMEM((PEER_TOPK, PEER_HEADS, tb), F32),
                        pltpu.VMEM((1, PEER_HEADS, tb), F32)],
        compiler_params=pltpu.CompilerParams(
            dimension_semantics=("arbitrary",),
            vmem_limit_bytes=_vmem_limit(blk)),
        name="peer_select",
    )(n2, w1, w2)


def _dense_kernel(sc_ref, n2_ref, u_ref, vt_ref, p_ref, c_ref, q_ref, r_ref, h_ref, fg_ref,
                  y_ref, acc_ref, at_ref):
    tb = DENSE_TB
    eb = pl.program_id(1)

    @pl.when(eb == 0)
    def _():
        acc_ref[...] = jnp.zeros_like(acc_ref)

    n2q = n2_ref[...]
    keys_per_chunk = DENSE_HC // PEER_NKEYS
    n_chunks = DENSE_EB // DENSE_HC
    slab = (PEER_NKEYS, 2 * V7X_LANES)
    k1 = jnp.full(slab, sc_ref[0], F32).astype(BF16)
    k3 = jnp.full(slab, sc_ref[1], F32).astype(BF16)

    def hidden(hc):
        return _dot(u_ref[hc * DENSE_HC:(hc + 1) * DENSE_HC, :], n2q).astype(BF16)

    def scaled_gelu(z):
        e = jnp.exp2(z * (k3 * (z * z) + k1))
        return z / (1.0 + e)

    def gate_rows(a, ht):
        rows = slice(a * PEER_NKEYS, (a + 1) * PEER_NKEYS)
        lrows = slice((a % keys_per_chunk) * PEER_NKEYS, (a % keys_per_chunk + 1) * PEER_NKEYS)
        for lp in range(tb // (2 * V7X_LANES)):
            ln = slice(lp * 2 * V7X_LANES, (lp + 1) * 2 * V7X_LANES)

            def key_row(ref, h):
                halves = [ref[h, 2 * lp + i, pl.ds(a, V7X_SUBLANES, stride=0), :] for i in (0, 1)]
                row = jnp.concatenate(halves, axis=1)
                row = jnp.concatenate([row, row], axis=0).astype(BF16)
                return pltpu.repeat(row, PEER_NKEYS // V7X_BF16_ROWS, 0)

            act = scaled_gelu(ht[lrows, ln])
            gate = None
            for h in range(PEER_HEADS):
                prow = key_row(p_ref, h)
                crow = key_row(c_ref, h)
                term = jnp.where(r_ref[h, :, ln] < crow, q_ref[h, :, ln] * prow,
                                 jnp.zeros((), BF16))
                gate = term if gate is None else gate + term
            at_ref[rows, ln] = act * gate

    for hc in range(n_chunks):
        ht = hidden(hc)
        for al in range(keys_per_chunk):
            gate_rows(hc * keys_per_chunk + al, ht)
    acc_ref[...] += _dot(vt_ref[...], at_ref[...])

    @pl.when(eb == pl.num_programs(1) - 1)
    def _():
        h2 = h_ref[...] + acc_ref[...].T * sc_ref[2]
        y = h2 * lax.rsqrt(jnp.mean(h2 * h2, axis=-1, keepdims=True) + EPS) * fg_ref[...]
        y_ref[...] = y


def _peer_dense(sc, n2, u, vt, p, c, q, r, h, fg):
    tb, eb = DENSE_TB, DENSE_EB
    keys = eb // PEER_NKEYS
    lane_tiles = tb // V7X_LANES
    blk = (tb * D_MODEL + eb * D_MODEL * (1 + 2) + 2 * PEER_HEADS * keys * tb * 4
           + 2 * PEER_HEADS * PEER_NKEYS * tb * 2 + 2 * tb * D_MODEL * 4)
    scratch = D_MODEL * tb * 4 + eb * tb * 2
    return pl.pallas_call(
        _dense_kernel,
        grid=(TOKENS // tb, PEER_EXPERTS // eb),
        in_specs=[
            pl.BlockSpec(memory_space=pltpu.SMEM),
            pl.BlockSpec((D_MODEL, tb), lambda t, e: (0, t)),
            pl.BlockSpec((eb, D_MODEL), lambda t, e: (e, 0)),
            pl.BlockSpec((D_MODEL, eb), lambda t, e: (0, e)),
            pl.BlockSpec((PEER_HEADS, lane_tiles, keys, V7X_LANES), lambda t, e: (0, t, e, 0)),
            pl.BlockSpec((PEER_HEADS, lane_tiles, keys, V7X_LANES), lambda t, e: (0, t, e, 0)),
            pl.BlockSpec((PEER_HEADS, PEER_NKEYS, tb), lambda t, e: (0, 0, t)),
            pl.BlockSpec((PEER_HEADS, PEER_NKEYS, tb), lambda t, e: (0, 0, t)),
            pl.BlockSpec((tb, D_MODEL), lambda t, e: (t, 0)),
            pl.BlockSpec((1, D_MODEL), lambda t, e: (0, 0)),
        ],
        out_specs=pl.BlockSpec((tb, D_MODEL), lambda t, e: (t, 0)),
        out_shape=jax.ShapeDtypeStruct((TOKENS, D_MODEL), F32),
        scratch_shapes=[pltpu.VMEM((D_MODEL, tb), F32), pltpu.VMEM((eb, tb), BF16)],
        compiler_params=pltpu.CompilerParams(
            dimension_semantics=("arbitrary", "arbitrary"),
            vmem_limit_bytes=_vmem_limit(blk + scratch // 2)),
        name="peer_dense",
    )(sc, n2, u, vt, p, c, q, r, h, fg)


def kernel(x, norm1_g, w_in, w_gate_up, b_gate, gla_norm_g, sgu_ln_g, sgu_ln_b, sgu_w, sgu_b,
           w_branch_a, w_branch_b, w_out, norm2_g, peer_wq, peer_k1, peer_k2, peer_u, peer_v,
           final_g):
    assert x.shape == (BATCH, SEQ, D_MODEL) and w_in.shape[0] == 1
    row = lambda a: a.reshape(1, -1).astype(F32)

    w = w_in[0]
    o_alr = 4 * GLA_W
    o_su = o_alr + GLA_LOWRANK
    o_sv = o_su + SGU_W
    o_ga = o_sv + SGU_W
    o_gb = o_ga + D_MODEL
    alr_pad = jnp.pad(w[:, o_alr:o_su], ((0, 0), (0, V7X_LANES - GLA_LOWRANK)))
    win = jnp.concatenate([w[:, :o_alr], w[:, o_su:o_sv], w[:, o_sv:o_ga], w[:, o_ga:o_gb],
                           w[:, o_gb:], alr_pad], axis=1).astype(BF16)
    wa2 = jnp.pad(w_gate_up[0], ((0, V7X_LANES - GLA_LOWRANK), (0, 0))).astype(BF16)
    sbb = jnp.broadcast_to(sgu_b[0][:, :, None], (SGU_GROUPS, SGU_BLOCK, SGU_BLOCK)).astype(F32)

    s_n = _pow2_scale(math.sqrt(D_MODEL) * jnp.max(jnp.abs(norm2_g[0])))
    s_u = _pow2_scale(jnp.max(jnp.abs(peer_u[0])))

    h, n2, n2q = _mixers(s_n.reshape(1).astype(F32), x, row(norm1_g[0]), win, wa2, row(b_gate[0]),
                         row(gla_norm_g[0]), row(sgu_ln_g[0]), row(sgu_ln_b[0]), sgu_w[0], sbb,
                         w_branch_a[0].astype(BF16), w_branch_b[0].astype(BF16),
                         w_out[0].astype(BF16), row(norm2_g[0]))
    h = h.reshape(TOKENS, D_MODEL)

    wfold = _peer_fold(jnp.stack([peer_k1[0], peer_k2[0]]), peer_wq[0])
    hk = PEER_HEADS * PEER_NKEYS
    p, c, q, r = _peer_select(n2, wfold[:hk], wfold[hk:])

    inv = 1.0 / (s_u * s_n)
    two_c_log2e = 2.0 * math.sqrt(2.0 / math.pi) * math.log2(math.e)
    sc = jnp.stack([-two_c_log2e * inv, -two_c_log2e * 0.044715 * inv * inv * inv, inv]).astype(F32)
    u = (peer_u[0] * s_u).astype(FP8)
    vt = peer_v[0].T.astype(BF16)
    y = _peer_dense(sc, n2q, u, vt, p, c, q, r, h, row(final_g))
    return y.reshape(BATCH, SEQ, D_MODEL)
```

```python
import functools
import math

import jax
import jax.numpy as jnp
import numpy as np
from jax import lax
from jax.experimental import pallas as pl
from jax.experimental.pallas import tpu as pltpu

F32 = jnp.float32
BF16 = jnp.bfloat16
FP8 = jnp.float8_e4m3fn
FP8_MAX = 448.0

D_MODEL = 1024
BATCH = 8
SEQ = 4096
TOKENS = BATCH * SEQ
CHUNK = 64
EPS = 1e-6
GLA_HEADS = 4
GLA_DK = 128
GLA_DV = 128
GLA_LOWRANK = 16
GLA_TAU = 16.0
GLA_W = GLA_HEADS * GLA_DK
SGU_GROUPS = 4
SGU_BLOCK = 128
SGU_W = 512
PEER_HEADS = 8
PEER_DKEY = 256
PEER_NKEYS = 128
PEER_TOPK = 16
PEER_EXPERTS = PEER_NKEYS * PEER_NKEYS

V7X_LANES = 128
V7X_SUBLANES = 8
V7X_BF16_ROWS = 16
V7X_VMEM_BYTES = 64 * 1024 * 1024

_C_QKVR = 0
_C_SU = 2048
_C_SV = 2560
_C_GA = 3072
_C_GB = 4096
_C_ALR = 5120
_IN_PACKED = 5248

MIX_SB = 256
SEL_TB = 256
DENSE_TB = 512
DENSE_EB = 4096
DENSE_HC = 1024

_NT = (((1,), (1,)), ((), ()))
_TN = (((0,), (0,)), ((), ()))


def _dot(a, b):
    return jnp.dot(a, b, preferred_element_type=F32)


def _dot_nt(a, b):
    return lax.dot_general(a, b, _NT, preferred_element_type=F32)


def _pow2_scale(bound):
    return jnp.exp2(jnp.floor(jnp.log2(0.5 * FP8_MAX / jnp.maximum(bound, 1e-30))))


def _vmem_limit(block_bytes):
    return int(min(V7X_VMEM_BYTES - 8 * 1024 * 1024, 2 * block_bytes + 16 * 1024 * 1024))


def _oddeven_merge_sort_pairs(n):
    pairs = []
    p = 1
    while p < n:
        k = p
        while k >= 1:
            for j in range(k % p, n - k, 2 * k):
                for i in range(min(k, n - j - k)):
                    if (i + j) // (2 * p) == (i + j + k) // (2 * p):
                        pairs.append((i + j, i + j + k))
            k //= 2
        p *= 2
    return pairs


_SORT16 = _oddeven_merge_sort_pairs(16)
_CLEAN16 = [(i, i + d) for d in (8, 4, 2, 1) for i in range(16) if not i & d]


def _cmpx(xs, i, j):
    a, b = xs[i], xs[j]
    if b is None:
        return
    if a is None:
        xs[i], xs[j] = b, None
        return
    xs[i] = jnp.maximum(a, b)
    xs[j] = jnp.minimum(a, b)


def _sort16(xs):
    xs = list(xs)
    for i, j in _SORT16:
        _cmpx(xs, i, j)
    return xs


def _merge_top16(a, b):
    z = []
    for k in range(16):
        x, y = a[k], b[15 - k]
        if y is None:
            z.append(x)
        elif x is None:
            z.append(y)
        else:
            z.append(jnp.maximum(x, y))
    for i, j in _CLEAN16:
        _cmpx(z, i, j)
    return z


def _top16_of(vals):
    vals = list(vals) + [None] * (-len(vals) % 16)
    groups = [_sort16(vals[g:g + 16]) for g in range(0, len(vals), 16)]
    while len(groups) > 1:
        nxt = [_merge_top16(groups[g], groups[g + 1]) for g in range(0, len(groups) - 1, 2)]
        if len(groups) % 2:
            nxt.append(groups[-1])
        groups = nxt
    return groups[0]


def _top16_rows(tiles):
    xs = _sort16(tiles)
    for shift in (4, 2, 1):
        rolled = [pltpu.roll(x, shift, 0) for x in xs]
        xs = _merge_top16(xs, rolled)
    return xs


def _fold_kernel(k_ref, wq_ref, o_ref):
    o_ref[...] = lax.dot_general(k_ref[0], wq_ref[...], _NT,
                                 precision=lax.Precision.HIGHEST,
                                 preferred_element_type=F32).astype(o_ref.dtype)


def _peer_fold(k12, wq):
    half = PEER_DKEY // 2
    return pl.pallas_call(
        _fold_kernel,
        grid=(2, PEER_HEADS),
        in_specs=[
            pl.BlockSpec((1, PEER_NKEYS, half), lambda s, h: (s, 0, 0)),
            pl.BlockSpec((D_MODEL, half), lambda s, h: (0, 2 * h + s)),
        ],
        out_specs=pl.BlockSpec((PEER_NKEYS, D_MODEL), lambda s, h: (s * PEER_HEADS + h, 0)),
        out_shape=jax.ShapeDtypeStruct((2 * PEER_HEADS * PEER_NKEYS, D_MODEL), BF16),
        name="peer_fold",
    )(k12, wq)


def _mix_kernel(sn_ref, x_ref, g1_ref, win_ref, wa2_ref, bg_ref, gno_ref, lng_ref, lnb_ref,
                sw_ref, sbb_ref, wba_ref, wbb_ref, wout_ref, g2_ref,
                h_ref, n2_ref, n2q_ref, st_ref, ya_ref, yb_ref, later_ref, wm_ref):
    sb = MIX_SB
    chunk_shift = CHUNK.bit_length() - 1

    @pl.when((pl.program_id(0) == 0) & (pl.program_id(1) == 0))
    def _():
        ri = lax.broadcasted_iota(jnp.int32, (sb, sb), 0)
        ci = lax.broadcasted_iota(jnp.int32, (sb, sb), 1)
        same = (ci >> chunk_shift) == (ri >> chunk_shift)
        later_ref[...] = jnp.where((ci > ri) & same, 1.0, 0.0).astype(BF16)
        bi = lax.broadcasted_iota(jnp.int32, (SGU_BLOCK, SGU_BLOCK), 0)
        bj = lax.broadcasted_iota(jnp.int32, (SGU_BLOCK, SGU_BLOCK), 1)
        causal = (bi >> chunk_shift) >= (bj >> chunk_shift)
        for g in range(SGU_GROUPS):
            wm_ref[g] = jnp.where(causal, sw_ref[g], 0.0).astype(BF16)

    @pl.when(pl.program_id(1) == 0)
    def _():
        st_ref[...] = jnp.zeros_like(st_ref)

    x = x_ref[0]
    n1 = x * lax.rsqrt(jnp.mean(x * x, axis=-1, keepdims=True) + EPS) * g1_ref[...]
    n1b = n1.astype(BF16)

    qkvr = _dot(n1b, win_ref[:, _C_QKVR:_C_QKVR + 4 * GLA_W])
    alr = _dot(n1b, win_ref[:, _C_ALR:_C_ALR + V7X_LANES])
    z = _dot(alr.astype(BF16), wa2_ref[...]) + bg_ref[...]
    su_pre = _dot(n1b, win_ref[:, _C_SU:_C_SU + SGU_W])
    sv_pre = _dot(n1b, win_ref[:, _C_SV:_C_SV + SGU_W])

    q = qkvr[:, 0:GLA_W] * (GLA_DK ** -0.5)
    k = qkvr[:, GLA_W:2 * GLA_W]
    v = qkvr[:, 2 * GLA_W:3 * GLA_W]
    r = qkvr[:, 3 * GLA_W:4 * GLA_W]
    la = (jnp.minimum(z, 0.0) - jnp.log1p(jnp.exp(-jnp.abs(z)))) * (1.0 / GLA_TAU)
    la_hi = la.astype(BF16)
    la_lo = (la - la_hi.astype(F32)).astype(BF16)
    dec = _dot(later_ref[...], la_hi) + _dot(later_ref[...], la_lo)
    ga_pre = _dot(n1b, win_ref[:, _C_GA:_C_GA + D_MODEL])
    gb_pre = _dot(n1b, win_ref[:, _C_GB:_C_GB + D_MODEL])

    kdec = (k * jnp.exp(dec)).astype(BF16)
    vb = v.astype(BF16)
    qb = q.astype(BF16)
    heads = [slice(hh * GLA_DK, (hh + 1) * GLA_DK) for hh in range(GLA_HEADS)]
    chunks = [slice(c * CHUNK, (c + 1) * CHUNK) for c in range(sb // CHUNK)]
    kv_t = [[lax.dot_general(vb[rs, ls], kdec[rs, ls], _TN, preferred_element_type=F32)
             for ls in heads] for rs in chunks]

    su = jax.nn.gelu(su_pre)
    sv = jax.nn.gelu(sv_pre)
    mu = jnp.mean(sv, axis=-1, keepdims=True)
    svc = sv - mu
    var = jnp.mean(svc * svc, axis=-1, keepdims=True)
    vn = (svc * lax.rsqrt(var + EPS) * lng_ref[...] + lnb_ref[...]).astype(BF16)
    for g in range(SGU_GROUPS):
        ls = slice(g * SGU_BLOCK, (g + 1) * SGU_BLOCK)
        for nb in range(sb // SGU_BLOCK):
            rs = slice(nb * SGU_BLOCK, (nb + 1) * SGU_BLOCK)
            mixed = _dot(wm_ref[g], vn[rs, ls]) + sbb_ref[g]
            yb_ref[rs, ls] = su[rs, ls] * mixed
    y_b = yb_ref[...]

    for hh, ls in enumerate(heads):
        st = st_ref[hh]
        for c, rs in enumerate(chunks):
            tot = dec[c * CHUNK:c * CHUNK + 1, ls] + la[c * CHUNK:c * CHUNK + 1, ls]
            st = st * jnp.exp(tot) + kv_t[c][hh]
            o = _dot_nt(qb[rs, ls], st.astype(BF16))
            o = o * lax.rsqrt(jnp.mean(o * o, axis=-1, keepdims=True) + EPS)
            ya_ref[rs, ls] = o
        st_ref[hh] = st
    y_a = ya_ref[...] * gno_ref[...] * (r * jax.nn.sigmoid(r))

    ga = jax.nn.sigmoid(ga_pre)
    gb = jax.nn.sigmoid(gb_pre)
    merged = ga * _dot(y_a.astype(BF16), wba_ref[...]) + gb * _dot(y_b.astype(BF16), wbb_ref[...])
    h = x + _dot(merged.astype(BF16), wout_ref[...])
    h_ref[0] = h
    n2 = h * lax.rsqrt(jnp.mean(h * h, axis=-1, keepdims=True) + EPS) * g2_ref[...]
    n2t = n2.T
    n2_ref[...] = n2t.astype(BF16)
    n2q_ref[...] = (n2t * sn_ref[0]).astype(FP8)


def _mixers(sn, x, g1, win, wa2, bg, gno, lng, lnb, sw, sbb, wba, wbb, wout, g2):
    sb = MIX_SB
    full = lambda a: pl.BlockSpec(a.shape, lambda b, s, _n=a.ndim: (0,) * _n)
    consts = (g1, win, wa2, bg, gno, lng, lnb, sw, sbb, wba, wbb, wout, g2)
    blk = sum(int(np.prod(a.shape)) * a.dtype.itemsize for a in consts)
    blk += sb * D_MODEL * (4 + 4 + 2 + 1)
    tspec = pl.BlockSpec((D_MODEL, sb), lambda b, s: (0, b * (SEQ // sb) + s))
    return pl.pallas_call(
        _mix_kernel,
        grid=(BATCH, SEQ // sb),
        in_specs=[pl.BlockSpec(memory_space=pltpu.SMEM),
                  pl.BlockSpec((1, sb, D_MODEL), lambda b, s: (b, s, 0))] + [full(a) for a in consts],
        out_specs=[pl.BlockSpec((1, sb, D_MODEL), lambda b, s: (b, s, 0)), tspec, tspec],
        out_shape=[jax.ShapeDtypeStruct((BATCH, SEQ, D_MODEL), F32),
                   jax.ShapeDtypeStruct((D_MODEL, TOKENS), BF16),
                   jax.ShapeDtypeStruct((D_MODEL, TOKENS), FP8)],
        scratch_shapes=[pltpu.VMEM((GLA_HEADS, GLA_DV, GLA_DK), F32),
                        pltpu.VMEM((sb, GLA_W), F32),
                        pltpu.VMEM((sb, SGU_W), F32),
                        pltpu.VMEM((sb, sb), BF16),
                        pltpu.VMEM((SGU_GROUPS, SGU_BLOCK, SGU_BLOCK), BF16)],
        compiler_params=pltpu.CompilerParams(
            dimension_semantics=("arbitrary", "arbitrary"),
            vmem_limit_bytes=_vmem_limit(blk)),
        name="mixers",
    )(sn, x, *consts)


def _prefix_count(pred, levels):
    assert len(levels) == PEER_TOPK == 16
    w = jnp.where
    m8 = pred(levels[7])
    m4 = pred(w(m8, levels[11], levels[3]))
    m2 = pred(w(m8, w(m4, levels[13], levels[9]), w(m4, levels[5], levels[1])))
    m1 = pred(w(m8,
                w(m4, w(m2, levels[14], levels[12]), w(m2, levels[10], levels[8])),
                w(m4, w(m2, levels[6], levels[4]), w(m2, levels[2], levels[0]))))
    m16 = pred(levels[15])
    return ((w(m8, 8.0, 0.0) + w(m4, 4.0, 0.0)) + (w(m2, 2.0, 0.0) + w(m1, 1.0, 0.0))
            + w(m16, 1.0, 0.0))


def _row(ref, idx, h, rows):
    return jnp.broadcast_to(ref[idx, pl.ds(h, 1), :], (rows, ref.shape[-1]))


def _select_kernel(n2_ref, w1_ref, w2_ref, p_ref, c_ref, q_ref, r_ref,
                   s1_ref, s2_ref, v1_ref, v2_ref, sg_ref, aux_ref):
    tb = SEL_TB
    n2t = n2_ref[...]
    s1_ref[...] = _dot(w1_ref[...], n2t)
    s2_ref[...] = _dot(w2_ref[...], n2t)

    def top_body(h, carry):
        base = pl.multiple_of(h * PEER_NKEYS, PEER_NKEYS)
        for s_ref, v_ref in ((s1_ref, v1_ref), (s2_ref, v2_ref)):
            tiles = [s_ref[pl.ds(base + V7X_SUBLANES * t, V7X_SUBLANES), :]
                     for t in range(PEER_NKEYS // V7X_SUBLANES)]
            top = _top16_rows(tiles)
            for i in range(PEER_TOPK):
                v_ref[i, pl.ds(h, 1), :] = top[i][0:1, :]
        return carry

    lax.fori_loop(0, PEER_HEADS, top_body, 0)

    v1 = [v1_ref[i] for i in range(PEER_TOPK)]
    v2 = [v2_ref[i] for i in range(PEER_TOPK)]
    cand = {(i, j): v1[i] + v2[j] for i in range(PEER_TOPK) for j in range(PEER_TOPK)
            if (i + 1) * (j + 1) <= PEER_TOPK}
    thr = _top16_of(list(cand.values()))[PEER_TOPK - 1]
    cmax = cand[(0, 0)]
    zsum = jnp.zeros((PEER_HEADS, tb), F32)
    for cnd in cand.values():
        zsum = zsum + jnp.where(cnd >= thr, jnp.exp(cnd - cmax), 0.0)
    aux_ref[0] = 1.0 / zsum
    for j in range(PEER_TOPK):
        sig = jnp.full((PEER_HEADS, tb), jnp.inf, F32)
        for i in range(PEER_TOPK):
            if (i, j) in cand:
                sig = jnp.where(cand[(i, j)] >= thr, v1[i], sig)
        sg_ref[j] = sig

    def emit_body(h, carry):
        base = pl.multiple_of(h * PEER_NKEYS, PEER_NKEYS)
        v2b = [_row(v2_ref, j, h, V7X_BF16_ROWS) for j in range(PEER_TOPK)]
        sgb = [_row(sg_ref, j, h, V7X_BF16_ROWS) for j in range(PEER_TOPK)]
        iz_b = _row(aux_ref, 0, h, V7X_BF16_ROWS)
        m1_b = _row(v1_ref, 0, h, V7X_BF16_ROWS)
        m2_b = v2b[0]
        for t in range(PEER_NKEYS // V7X_BF16_ROWS):
            rs = pl.ds(base + V7X_BF16_ROWS * t, V7X_BF16_ROWS)
            os = pl.ds(V7X_BF16_ROWS * t, V7X_BF16_ROWS)
            s1t = s1_ref[rs, :]
            s2t = s2_ref[rs, :]
            cnt = _prefix_count(lambda lvl: s1t >= lvl, sgb)
            rank = _prefix_count(lambda lvl: lvl > s2t, v2b)
            pexp = jnp.exp(s1t - m1_b)
            for lt in range(tb // V7X_LANES):
                ln = slice(lt * V7X_LANES, (lt + 1) * V7X_LANES)
                c_ref[h, lt, os, :] = cnt[:, ln]
                p_ref[h, lt, os, :] = pexp[:, ln]
            r_ref[h, os, :] = rank.astype(BF16)
            q_ref[h, os, :] = (jnp.exp(s2t - m2_b) * iz_b).astype(BF16)
        return carry

    lax.fori_loop(0, PEER_HEADS, emit_body, 0)


def _peer_select(n2, w1, w2):
    tb = SEL_TB
    hk = PEER_HEADS * PEER_NKEYS
    sel = lambda dt: jax.ShapeDtypeStruct((PEER_HEADS, PEER_NKEYS, TOKENS), dt)
    ospec = pl.BlockSpec((PEER_HEADS, PEER_NKEYS, tb), lambda i: (0, 0, i))
    rows = jax.ShapeDtypeStruct((PEER_HEADS, TOKENS // V7X_LANES, PEER_NKEYS, V7X_LANES), F32)
    rspec = pl.BlockSpec((PEER_HEADS, tb // V7X_LANES, PEER_NKEYS, V7X_LANES),
                         lambda i: (0, i, 0, 0))
    blk = tb * D_MODEL * 2 + 2 * hk * D_MODEL * 2 + hk * tb * (4 + 4 + 2 + 2)
    return pl.pallas_call(
        _select_kernel,
        grid=(TOKENS // tb,),
        in_specs=[pl.BlockSpec((D_MODEL, tb), lambda i: (0, i)),
                  pl.BlockSpec((hk, D_MODEL), lambda i: (0, 0)),
                  pl.BlockSpec((hk, D_MODEL), lambda i: (0, 0))],
        out_specs=[rspec, rspec, ospec, ospec],
        out_shape=[rows, rows, sel(BF16), sel(BF16)],
        scratch_shapes=[pltpu.VMEM((hk, tb), F32), pltpu.VMEM((hk, tb), F32),
                        pltpu.VMEM((PEER_TOPK, PEER_HEADS, tb), F32),
                        pltpu.VMEM((PEER_TOPK, PEER_HEADS, tb), F32),
                        pltpu.VMEM((PEER_TOPK, PEER_HEADS, tb), F32),
                        pltpu.VMEM((1, PEER_HEADS, tb), F32)],
        compiler_params=pltpu.CompilerParams(
            dimension_semantics=("arbitrary",),
            vmem_limit_bytes=_vmem_limit(blk)),
        name="peer_select",
    )(n2, w1, w2)


def _dense_kernel(sc_ref, n2_ref, u_ref, vt_ref, p_ref, c_ref, q_ref, r_ref, h_ref, fg_ref,
                  y_ref, acc_ref, at_ref):
    tb = DENSE_TB
    eb = pl.program_id(1)

    @pl.when(eb == 0)
    def _():
        acc_ref[...] = jnp.zeros_like(acc_ref)

    n2q = n2_ref[...]
    keys_per_chunk = DENSE_HC // PEER_NKEYS
    n_chunks = DENSE_EB // DENSE_HC
    slab = (PEER_NKEYS, 2 * V7X_LANES)
    k1 = jnp.full(slab, sc_ref[0], F32).astype(BF16)
    k3 = jnp.full(slab, sc_ref[1], F32).astype(BF16)

    def hidden(hc):
        return _dot(u_ref[hc * DENSE_HC:(hc + 1) * DENSE_HC, :], n2q).astype(BF16)

    def scaled_gelu(z):
        e = jnp.exp2(z * (k3 * (z * z) + k1))
        return z / (1.0 + e)

    slabs = [slice(lp * 2 * V7X_LANES, (lp + 1) * 2 * V7X_LANES)
             for lp in range(tb // (2 * V7X_LANES))]

    def gate_rows(a):
        rows = slice(a * PEER_NKEYS, (a + 1) * PEER_NKEYS)
        for lp, ln in enumerate(slabs):

            def key_row(ref, h):
                halves = [ref[h, 2 * lp + i, pl.ds(a, V7X_SUBLANES, stride=0), :] for i in (0, 1)]
                row = jnp.concatenate(halves, axis=1)
                row = jnp.concatenate([row, row], axis=0).astype(BF16)
                return pltpu.repeat(row, PEER_NKEYS // V7X_BF16_ROWS, 0)

            gate = None
            for h in range(PEER_HEADS):
                prow = key_row(p_ref, h)
                crow = key_row(c_ref, h)
                term = jnp.where(r_ref[h, :, ln] < crow, q_ref[h, :, ln] * prow,
                                 jnp.zeros((), BF16))
                gate = term if gate is None else gate + term
            at_ref[rows, ln] = gate

    for a in range(DENSE_EB // PEER_NKEYS):
        gate_rows(a)
    for hc in range(n_chunks):
        ht = hidden(hc)
        for al in range(keys_per_chunk):
            rows = slice((hc * keys_per_chunk + al) * PEER_NKEYS,
                         (hc * keys_per_chunk + al + 1) * PEER_NKEYS)
            lrows = slice(al * PEER_NKEYS, (al + 1) * PEER_NKEYS)
            for ln in slabs:
                at_ref[rows, ln] = scaled_gelu(ht[lrows, ln]) * at_ref[rows, ln]
    acc_ref[...] += _dot(vt_ref[...], at_ref[...])

    @pl.when(eb == pl.num_programs(1) - 1)
    def _():
        h2 = h_ref[...] + acc_ref[...].T * sc_ref[2]
        y = h2 * lax.rsqrt(jnp.mean(h2 * h2, axis=-1, keepdims=True) + EPS) * fg_ref[...]
        y_ref[...] = y


def _peer_dense(sc, n2, u, vt, p, c, q, r, h, fg):
    tb, eb = DENSE_TB, DENSE_EB
    keys = eb // PEER_NKEYS
    lane_tiles = tb // V7X_LANES
    blk = (tb * D_MODEL + eb * D_MODEL * (1 + 2) + 2 * PEER_HEADS * keys * tb * 4
           + 2 * PEER_HEADS * PEER_NKEYS * tb * 2 + 2 * tb * D_MODEL * 4)
    scratch = D_MODEL * tb * 4 + eb * tb * 2
    return pl.pallas_call(
        _dense_kernel,
        grid=(TOKENS // tb, PEER_EXPERTS // eb),
        in_specs=[
            pl.BlockSpec(memory_space=pltpu.SMEM),
            pl.BlockSpec((D_MODEL, tb), lambda t, e: (0, t)),
            pl.BlockSpec((eb, D_MODEL), lambda t, e: (e, 0)),
            pl.BlockSpec((D_MODEL, eb), lambda t, e: (0, e)),
            pl.BlockSpec((PEER_HEADS, lane_tiles, keys, V7X_LANES), lambda t, e: (0, t, e, 0)),
            pl.BlockSpec((PEER_HEADS, lane_tiles, keys, V7X_LANES), lambda t, e: (0, t, e, 0)),
            pl.BlockSpec((PEER_HEADS, PEER_NKEYS, tb), lambda t, e: (0, 0, t)),
            pl.BlockSpec((PEER_HEADS, PEER_NKEYS, tb), lambda t, e: (0, 0, t)),
            pl.BlockSpec((tb, D_MODEL), lambda t, e: (t, 0)),
            pl.BlockSpec((1, D_MODEL), lambda t, e: (0, 0)),
        ],
        out_specs=pl.BlockSpec((tb, D_MODEL), lambda t, e: (t, 0)),
        out_shape=jax.ShapeDtypeStruct((TOKENS, D_MODEL), F32),
        scratch_shapes=[pltpu.VMEM((D_MODEL, tb), F32), pltpu.VMEM((eb, tb), BF16)],
        compiler_params=pltpu.CompilerParams(
            dimension_semantics=("arbitrary", "arbitrary"),
            vmem_limit_bytes=_vmem_limit(blk + scratch // 2)),
        name="peer_dense",
    )(sc, n2, u, vt, p, c, q, r, h, fg)


def kernel(x, norm1_g, w_in, w_gate_up, b_gate, gla_norm_g, sgu_ln_g, sgu_ln_b, sgu_w, sgu_b,
           w_branch_a, w_branch_b, w_out, norm2_g, peer_wq, peer_k1, peer_k2, peer_u, peer_v,
           final_g):
    assert x.shape == (BATCH, SEQ, D_MODEL) and w_in.shape[0] == 1
    row = lambda a: a.reshape(1, -1).astype(F32)

    w = w_in[0]
    o_alr = 4 * GLA_W
    o_su = o_alr + GLA_LOWRANK
    o_sv = o_su + SGU_W
    o_ga = o_sv + SGU_W
    o_gb = o_ga + D_MODEL
    alr_pad = jnp.pad(w[:, o_alr:o_su], ((0, 0), (0, V7X_LANES - GLA_LOWRANK)))
    win = jnp.concatenate([w[:, :o_alr], w[:, o_su:o_sv], w[:, o_sv:o_ga], w[:, o_ga:o_gb],
                           w[:, o_gb:], alr_pad], axis=1).astype(BF16)
    wa2 = jnp.pad(w_gate_up[0], ((0, V7X_LANES - GLA_LOWRANK), (0, 0))).astype(BF16)
    sbb = jnp.broadcast_to(sgu_b[0][:, :, None], (SGU_GROUPS, SGU_BLOCK, SGU_BLOCK)).astype(F32)

    s_n = _pow2_scale(math.sqrt(D_MODEL) * jnp.max(jnp.abs(norm2_g[0])))
    s_u = _pow2_scale(jnp.max(jnp.abs(peer_u[0])))

    h, n2, n2q = _mixers(s_n.reshape(1).astype(F32), x, row(norm1_g[0]), win, wa2, row(b_gate[0]),
                         row(gla_norm_g[0]), row(sgu_ln_g[0]), row(sgu_ln_b[0]), sgu_w[0], sbb,
                         w_branch_a[0].astype(BF16), w_branch_b[0].astype(BF16),
                         w_out[0].astype(BF16), row(norm2_g[0]))
    h = h.reshape(TOKENS, D_MODEL)

    wfold = _peer_fold(jnp.stack([peer_k1[0], peer_k2[0]]), peer_wq[0])
    hk = PEER_HEADS * PEER_NKEYS
    p, c, q, r = _peer_select(n2, wfold[:hk], wfold[hk:])

    inv = 1.0 / (s_u * s_n)
    two_c_log2e = 2.0 * math.sqrt(2.0 / math.pi) * math.log2(math.e)
    sc = jnp.stack([-two_c_log2e * inv, -two_c_log2e * 0.044715 * inv * inv * inv, inv]).astype(F32)
    u = (peer_u[0] * s_u).astype(FP8)
    vt = peer_v[0].T.astype(BF16)
    y = _peer_dense(sc, n2q, u, vt, p, c, q, r, h, row(final_g))
    return y.reshape(BATCH, SEQ, D_MODEL)
```

```python
import functools
import math

import jax
import jax.numpy as jnp
import numpy as np
from jax import lax
from jax.experimental import pallas as pl
from jax.experimental.pallas import tpu as pltpu

F32 = jnp.float32
BF16 = jnp.bfloat16
FP8 = jnp.float8_e4m3fn
FP8_MAX = 448.0

D_MODEL = 1024
BATCH = 8
SEQ = 4096
TOKENS = BATCH * SEQ
CHUNK = 64
EPS = 1e-6
GLA_HEADS = 4
GLA_DK = 128
GLA_DV = 128
GLA_LOWRANK = 16
GLA_TAU = 16.0
GLA_W = GLA_HEADS * GLA_DK
SGU_GROUPS = 4
SGU_BLOCK = 128
SGU_W = 512
PEER_HEADS = 8
PEER_DKEY = 256
PEER_NKEYS = 128
PEER_TOPK = 16
PEER_EXPERTS = PEER_NKEYS * PEER_NKEYS

V7X_LANES = 128
V7X_SUBLANES = 8
V7X_BF16_ROWS = 16
V7X_VMEM_BYTES = 64 * 1024 * 1024

_C_QKVR = 0
_C_SU = 2048
_C_SV = 2560
_C_GA = 3072
_C_GB = 4096
_C_ALR = 5120
_IN_PACKED = 5248

MIX_SB = 256
SEL_TB = 256
DENSE_TB = 512
DENSE_EB = 4096
DENSE_HC = 1024

_NT = (((1,), (1,)), ((), ()))
_TN = (((0,), (0,)), ((), ()))


def _dot(a, b):
    return jnp.dot(a, b, preferred_element_type=F32)


def _dot_nt(a, b):
    return lax.dot_general(a, b, _NT, preferred_element_type=F32)


def _pow2_scale(bound):
    return jnp.exp2(jnp.floor(jnp.log2(0.5 * FP8_MAX / jnp.maximum(bound, 1e-30))))


def _vmem_limit(block_bytes):
    return int(min(V7X_VMEM_BYTES - 8 * 1024 * 1024, 2 * block_bytes + 16 * 1024 * 1024))


def _oddeven_merge_sort_pairs(n):
    pairs = []
    p = 1
    while p < n:
        k = p
        while k >= 1:
            for j in range(k % p, n - k, 2 * k):
                for i in range(min(k, n - j - k)):
                    if (i + j) // (2 * p) == (i + j + k) // (2 * p):
                        pairs.append((i + j, i + j + k))
            k //= 2
        p *= 2
    return pairs


_SORT16 = _oddeven_merge_sort_pairs(16)
_CLEAN16 = [(i, i + d) for d in (8, 4, 2, 1) for i in range(16) if not i & d]


def _cmpx(xs, i, j):
    a, b = xs[i], xs[j]
    if b is None:
        return
    if a is None:
        xs[i], xs[j] = b, None
        return
    xs[i] = jnp.maximum(a, b)
    xs[j] = jnp.minimum(a, b)


def _sort16(xs):
    xs = list(xs)
    for i, j in _SORT16:
        _cmpx(xs, i, j)
    return xs


def _merge_top16(a, b):
    z = []
    for k in range(16):
        x, y = a[k], b[15 - k]
        if y is None:
            z.append(x)
        elif x is None:
            z.append(y)
        else:
            z.append(jnp.maximum(x, y))
    for i, j in _CLEAN16:
        _cmpx(z, i, j)
    return z


def _top16_of(vals):
    vals = list(vals) + [None] * (-len(vals) % 16)
    groups = [_sort16(vals[g:g + 16]) for g in range(0, len(vals), 16)]
    while len(groups) > 1:
        nxt = [_merge_top16(groups[g], groups[g + 1]) for g in range(0, len(groups) - 1, 2)]
        if len(groups) % 2:
            nxt.append(groups[-1])
        groups = nxt
    return groups[0]


def _top16_rows(tiles):
    xs = _sort16(tiles)
    for shift in (4, 2, 1):
        rolled = [pltpu.roll(x, shift, 0) for x in xs]
        xs = _merge_top16(xs, rolled)
    return xs


def _fold_kernel(k_ref, wq_ref, o_ref):
    o_ref[...] = lax.dot_general(k_ref[0], wq_ref[...], _NT,
                                 precision=lax.Precision.HIGHEST,
                                 preferred_element_type=F32).astype(o_ref.dtype)


def _peer_fold(k12, wq):
    half = PEER_DKEY // 2
    return pl.pallas_call(
        _fold_kernel,
        grid=(2, PEER_HEADS),
        in_specs=[
            pl.BlockSpec((1, PEER_NKEYS, half), lambda s, h: (s, 0, 0)),
            pl.BlockSpec((D_MODEL, half), lambda s, h: (0, 2 * h + s)),
        ],
        out_specs=pl.BlockSpec((PEER_NKEYS, D_MODEL), lambda s, h: (s * PEER_HEADS + h, 0)),
        out_shape=jax.ShapeDtypeStruct((2 * PEER_HEADS * PEER_NKEYS, D_MODEL), BF16),
        name="peer_fold",
    )(k12, wq)


def _mix_kernel(sn_ref, x_ref, g1_ref, win_ref, wa2_ref, bg_ref, gno_ref, lng_ref, lnb_ref,
                sw_ref, sbb_ref, wba_ref, wbb_ref, wout_ref, g2_ref,
                h_ref, n2_ref, n2q_ref, st_ref, ya_ref, yb_ref, later_ref, wm_ref):
    sb = MIX_SB
    chunk_shift = CHUNK.bit_length() - 1

    @pl.when((pl.program_id(0) == 0) & (pl.program_id(1) == 0))
    def _():
        ri = lax.broadcasted_iota(jnp.int32, (sb, sb), 0)
        ci = lax.broadcasted_iota(jnp.int32, (sb, sb), 1)
        same = (ci >> chunk_shift) == (ri >> chunk_shift)
        later_ref[...] = jnp.where((ci > ri) & same, 1.0, 0.0).astype(BF16)
        bi = lax.broadcasted_iota(jnp.int32, (SGU_BLOCK, SGU_BLOCK), 0)
        bj = lax.broadcasted_iota(jnp.int32, (SGU_BLOCK, SGU_BLOCK), 1)
        causal = (bi >> chunk_shift) >= (bj >> chunk_shift)
        for g in range(SGU_GROUPS):
            wm_ref[g] = jnp.where(causal, sw_ref[g], 0.0).astype(BF16)

    @pl.when(pl.program_id(1) == 0)
    def _():
        st_ref[...] = jnp.zeros_like(st_ref)

    x = x_ref[0]
    n1 = x * lax.rsqrt(jnp.mean(x * x, axis=-1, keepdims=True) + EPS) * g1_ref[...]
    n1b = n1.astype(BF16)

    qkvr = _dot(n1b, win_ref[:, _C_QKVR:_C_QKVR + 4 * GLA_W])
    alr = _dot(n1b, win_ref[:, _C_ALR:_C_ALR + V7X_LANES])
    z = _dot(alr.astype(BF16), wa2_ref[...]) + bg_ref[...]
    su_pre = _dot(n1b, win_ref[:, _C_SU:_C_SU + SGU_W])
    sv_pre = _dot(n1b, win_ref[:, _C_SV:_C_SV + SGU_W])

    q = qkvr[:, 0:GLA_W] * (GLA_DK ** -0.5)
    k = qkvr[:, GLA_W:2 * GLA_W]
    v = qkvr[:, 2 * GLA_W:3 * GLA_W]
    r = qkvr[:, 3 * GLA_W:4 * GLA_W]
    la = (jnp.minimum(z, 0.0) - jnp.log1p(jnp.exp(-jnp.abs(z)))) * (1.0 / GLA_TAU)
    la_hi = la.astype(BF16)
    la_lo = (la - la_hi.astype(F32)).astype(BF16)
    dec = _dot(later_ref[...], la_hi) + _dot(later_ref[...], la_lo)
    ga_pre = _dot(n1b, win_ref[:, _C_GA:_C_GA + D_MODEL])
    gb_pre = _dot(n1b, win_ref[:, _C_GB:_C_GB + D_MODEL])

    kdec = (k * jnp.exp(dec)).astype(BF16)
    vb = v.astype(BF16)
    qb = q.astype(BF16)
    heads = [slice(hh * GLA_DK, (hh + 1) * GLA_DK) for hh in range(GLA_HEADS)]
    chunks = [slice(c * CHUNK, (c + 1) * CHUNK) for c in range(sb // CHUNK)]
    kv_t = [[lax.dot_general(vb[rs, ls], kdec[rs, ls], _TN, preferred_element_type=F32)
             for ls in heads] for rs in chunks]

    su = jax.nn.gelu(su_pre)
    sv = jax.nn.gelu(sv_pre)
    mu = jnp.mean(sv, axis=-1, keepdims=True)
    svc = sv - mu
    var = jnp.mean(svc * svc, axis=-1, keepdims=True)
    vn = (svc * lax.rsqrt(var + EPS) * lng_ref[...] + lnb_ref[...]).astype(BF16)
    for g in range(SGU_GROUPS):
        ls = slice(g * SGU_BLOCK, (g + 1) * SGU_BLOCK)
        for nb in range(sb // SGU_BLOCK):
            rs = slice(nb * SGU_BLOCK, (nb + 1) * SGU_BLOCK)
            mixed = _dot(wm_ref[g], vn[rs, ls]) + sbb_ref[g]
            yb_ref[rs, ls] = su[rs, ls] * mixed
    y_b = yb_ref[...]

    for hh, ls in enumerate(heads):
        st = st_ref[hh]
        for c, rs in enumerate(chunks):
            tot = dec[c * CHUNK:c * CHUNK + 1, ls] + la[c * CHUNK:c * CHUNK + 1, ls]
            st = st * jnp.exp(tot) + kv_t[c][hh]
            o = _dot_nt(qb[rs, ls], st.astype(BF16))
            o = o * lax.rsqrt(jnp.mean(o * o, axis=-1, keepdims=True) + EPS)
            ya_ref[rs, ls] = o
        st_ref[hh] = st
    y_a = ya_ref[...] * gno_ref[...] * (r * jax.nn.sigmoid(r))

    ga = jax.nn.sigmoid(ga_pre)
    gb = jax.nn.sigmoid(gb_pre)
    merged = ga * _dot(y_a.astype(BF16), wba_ref[...]) + gb * _dot(y_b.astype(BF16), wbb_ref[...])
    h = x + _dot(merged.astype(BF16), wout_ref[...])
    h_ref[0] = h
    n2 = h * lax.rsqrt(jnp.mean(h * h, axis=-1, keepdims=True) + EPS) * g2_ref[...]
    n2t = n2.T
    n2_ref[...] = n2t.astype(BF16)
    n2q_ref[...] = (n2t * sn_ref[0]).astype(FP8)


def _mixers(sn, x, g1, win, wa2, bg, gno, lng, lnb, sw, sbb, wba, wbb, wout, g2):
    sb = MIX_SB
    full = lambda a: pl.BlockSpec(a.shape, lambda b, s, _n=a.ndim: (0,) * _n)
    consts = (g1, win, wa2, bg, gno, lng, lnb, sw, sbb, wba, wbb, wout, g2)
    blk = sum(int(np.prod(a.shape)) * a.dtype.itemsize for a in consts)
    blk += sb * D_MODEL * (4 + 4 + 2 + 1)
    tspec = pl.BlockSpec((D_MODEL, sb), lambda b, s: (0, b * (SEQ // sb) + s))
    return pl.pallas_call(
        _mix_kernel,
        grid=(BATCH, SEQ // sb),
        in_specs=[pl.BlockSpec(memory_space=pltpu.SMEM),
                  pl.BlockSpec((1, sb, D_MODEL), lambda b, s: (b, s, 0))] + [full(a) for a in consts],
        out_specs=[pl.BlockSpec((1, sb, D_MODEL), lambda b, s: (b, s, 0)), tspec, tspec],
        out_shape=[jax.ShapeDtypeStruct((BATCH, SEQ, D_MODEL), F32),
                   jax.ShapeDtypeStruct((D_MODEL, TOKENS), BF16),
                   jax.ShapeDtypeStruct((D_MODEL, TOKENS), FP8)],
        scratch_shapes=[pltpu.VMEM((GLA_HEADS, GLA_DV, GLA_DK), F32),
                        pltpu.VMEM((sb, GLA_W), F32),
                        pltpu.VMEM((sb, SGU_W), F32),
                        pltpu.VMEM((sb, sb), BF16),
                        pltpu.VMEM((SGU_GROUPS, SGU_BLOCK, SGU_BLOCK), BF16)],
        compiler_params=pltpu.CompilerParams(
            dimension_semantics=("arbitrary", "arbitrary"),
            vmem_limit_bytes=_vmem_limit(blk)),
        name="mixers",
    )(sn, x, *consts)


def _prefix_count(pred, levels):
    assert len(levels) == PEER_TOPK == 16
    w = jnp.where
    m8 = pred(levels[7])
    m4 = pred(w(m8, levels[11], levels[3]))
    m2 = pred(w(m8, w(m4, levels[13], levels[9]), w(m4, levels[5], levels[1])))
    m1 = pred(w(m8,
                w(m4, w(m2, levels[14], levels[12]), w(m2, levels[10], levels[8])),
                w(m4, w(m2, levels[6], levels[4]), w(m2, levels[2], levels[0]))))
    m16 = pred(levels[15])
    return ((w(m8, 8.0, 0.0) + w(m4, 4.0, 0.0)) + (w(m2, 2.0, 0.0) + w(m1, 1.0, 0.0))
            + w(m16, 1.0, 0.0))


def _row(ref, idx, h, rows):
    return jnp.broadcast_to(ref[idx, pl.ds(h, 1), :], (rows, ref.shape[-1]))


def _select_kernel(n2_ref, w1_ref, w2_ref, p_ref, c_ref, q_ref, r_ref,
                   s1_ref, s2_ref, v1_ref, v2_ref, sg_ref, aux_ref):
    tb = SEL_TB
    n2t = n2_ref[...]
    s1_ref[...] = _dot(w1_ref[...], n2t)
    s2_ref[...] = _dot(w2_ref[...], n2t)

    def top_body(h, carry):
        base = pl.multiple_of(h * PEER_NKEYS, PEER_NKEYS)
        for s_ref, v_ref in ((s1_ref, v1_ref), (s2_ref, v2_ref)):
            tiles = [s_ref[pl.ds(base + V7X_SUBLANES * t, V7X_SUBLANES), :]
                     for t in range(PEER_NKEYS // V7X_SUBLANES)]
            top = _top16_rows(tiles)
            for i in range(PEER_TOPK):
                v_ref[i, pl.ds(h, 1), :] = top[i][0:1, :]
        return carry

    lax.fori_loop(0, PEER_HEADS, top_body, 0)

    v1 = [v1_ref[i] for i in range(PEER_TOPK)]
    v2 = [v2_ref[i] for i in range(PEER_TOPK)]
    cand = {(i, j): v1[i] + v2[j] for i in range(PEER_TOPK) for j in range(PEER_TOPK)
            if (i + 1) * (j + 1) <= PEER_TOPK}
    thr = _top16_of(list(cand.values()))[PEER_TOPK - 1]
    cmax = cand[(0, 0)]
    zsum = jnp.zeros((PEER_HEADS, tb), F32)
    for cnd in cand.values():
        zsum = zsum + jnp.where(cnd >= thr, jnp.exp(cnd - cmax), 0.0)
    aux_ref[0] = 1.0 / zsum
    for j in range(PEER_TOPK):
        sig = jnp.full((PEER_HEADS, tb), jnp.inf, F32)
        for i in range(PEER_TOPK):
            if (i, j) in cand:
                sig = jnp.where(cand[(i, j)] >= thr, v1[i], sig)
        sg_ref[j] = sig

    def emit_body(h, carry):
        base = pl.multiple_of(h * PEER_NKEYS, PEER_NKEYS)
        v2b = [_row(v2_ref, j, h, V7X_BF16_ROWS) for j in range(PEER_TOPK)]
        sgb = [_row(sg_ref, j, h, V7X_BF16_ROWS) for j in range(PEER_TOPK)]
        iz_b = _row(aux_ref, 0, h, V7X_BF16_ROWS)
        m1_b = _row(v1_ref, 0, h, V7X_BF16_ROWS)
        m2_b = v2b[0]
        for t in range(PEER_NKEYS // V7X_BF16_ROWS):
            rs = pl.ds(base + V7X_BF16_ROWS * t, V7X_BF16_ROWS)
            os = pl.ds(V7X_BF16_ROWS * t, V7X_BF16_ROWS)
            s1t = s1_ref[rs, :]
            s2t = s2_ref[rs, :]
            cnt = _prefix_count(lambda lvl: s1t >= lvl, sgb)
            rank = _prefix_count(lambda lvl: lvl > s2t, v2b)
            pexp = jnp.exp(s1t - m1_b)
            for lt in range(tb // V7X_LANES):
                ln = slice(lt * V7X_LANES, (lt + 1) * V7X_LANES)
                c_ref[h, lt, os, :] = cnt[:, ln]
                p_ref[h, lt, os, :] = pexp[:, ln]
            r_ref[h, os, :] = rank.astype(BF16)
            q_ref[h, os, :] = (jnp.exp(s2t - m2_b) * iz_b).astype(BF16)
        return carry

    lax.fori_loop(0, PEER_HEADS, emit_body, 0)


def _peer_select(n2, w1, w2):
    tb = SEL_TB
    hk = PEER_HEADS * PEER_NKEYS
    sel = lambda dt: jax.ShapeDtypeStruct((PEER_HEADS, PEER_NKEYS, TOKENS), dt)
    ospec = pl.BlockSpec((PEER_HEADS, PEER_NKEYS, tb), lambda i: (0, 0, i))
    rows = jax.ShapeDtypeStruct((PEER_HEADS, TOKENS // V7X_LANES, PEER_NKEYS, V7X_LANES), F32)
    rspec = pl.BlockSpec((PEER_HEADS, tb // V7X_LANES, PEER_NKEYS, V7X_LANES),
                         lambda i: (0, i, 0, 0))
    blk = tb * D_MODEL * 2 + 2 * hk * D_MODEL * 2 + hk * tb * (4 + 4 + 2 + 2)
    return pl.pallas_call(
        _select_kernel,
        grid=(TOKENS // tb,),
        in_specs=[pl.BlockSpec((D_MODEL, tb), lambda i: (0, i)),
                  pl.BlockSpec((hk, D_MODEL), lambda i: (0, 0)),
                  pl.BlockSpec((hk, D_MODEL), lambda i: (0, 0))],
        out_specs=[rspec, rspec, ospec, ospec],
        out_shape=[rows, rows, sel(BF16), sel(BF16)],
        scratch_shapes=[pltpu.VMEM((hk, tb), F32), pltpu.VMEM((hk, tb), F32),
                        pltpu.VMEM((PEER_TOPK, PEER_HEADS, tb), F32),
                        pltpu.VMEM((PEER_TOPK, PEER_HEADS, tb), F32),
                        pltpu.VMEM((PEER_TOPK, PEER_HEADS, tb), F32),
                        pltpu.VMEM((1, PEER_HEADS, tb), F32)],
        compiler_params=pltpu.CompilerParams(
            dimension_semantics=("arbitrary",),
            vmem_limit_bytes=_vmem_limit(blk)),
        name="peer_select",
    )(n2, w1, w2)


def _dense_kernel(sc_ref, n2_ref, u_ref, vt_ref, p_ref, c_ref, q_ref, r_ref, h_ref, fg_ref,
                  y_ref, acc_ref, at_ref):
    tb = DENSE_TB
    eb = pl.program_id(1)

    @pl.when(eb == 0)
    def _():
        acc_ref[...] = jnp.zeros_like(acc_ref)

    n2q = n2_ref[...]
    keys_per_chunk = DENSE_HC // PEER_NKEYS
    n_chunks = DENSE_EB // DENSE_HC
    slab = (PEER_NKEYS, 2 * V7X_LANES)
    k1 = jnp.full(slab, sc_ref[0], F32).astype(BF16)
    k3 = jnp.full(slab, sc_ref[1], F32).astype(BF16)

    def hidden(hc):
        return _dot(u_ref[hc * DENSE_HC:(hc + 1) * DENSE_HC, :], n2q).astype(BF16)

    def scaled_gelu(z):
        return z * (1.0 + jnp.tanh(z * (k3 * (z * z) + k1)))

    slabs = [slice(lp * 2 * V7X_LANES, (lp + 1) * 2 * V7X_LANES)
             for lp in range(tb // (2 * V7X_LANES))]

    def gate_rows(a):
        rows = slice(a * PEER_NKEYS, (a + 1) * PEER_NKEYS)
        for lp, ln in enumerate(slabs):

            def key_row(ref, h):
                halves = [ref[h, 2 * lp + i, pl.ds(a, V7X_SUBLANES, stride=0), :] for i in (0, 1)]
                row = jnp.concatenate(halves, axis=1)
                row = jnp.concatenate([row, row], axis=0).astype(BF16)
                return pltpu.repeat(row, PEER_NKEYS // V7X_BF16_ROWS, 0)

            gate = None
            for h in range(PEER_HEADS):
                prow = key_row(p_ref, h)
                crow = key_row(c_ref, h)
                term = jnp.where(r_ref[h, :, ln] < crow, q_ref[h, :, ln] * prow,
                                 jnp.zeros((), BF16))
                gate = term if gate is None else gate + term
            at_ref[rows, ln] = gate

    for a in range(DENSE_EB // PEER_NKEYS):
        gate_rows(a)
    for hc in range(n_chunks):
        ht = hidden(hc)
        for al in range(keys_per_chunk):
            rows = slice((hc * keys_per_chunk + al) * PEER_NKEYS,
                         (hc * keys_per_chunk + al + 1) * PEER_NKEYS)
            lrows = slice(al * PEER_NKEYS, (al + 1) * PEER_NKEYS)
            for ln in slabs:
                at_ref[rows, ln] = scaled_gelu(ht[lrows, ln]) * at_ref[rows, ln]
    acc_ref[...] += _dot(vt_ref[...], at_ref[...])

    @pl.when(eb == pl.num_programs(1) - 1)
    def _():
        h2 = h_ref[...] + acc_ref[...].T * sc_ref[2]
        y = h2 * lax.rsqrt(jnp.mean(h2 * h2, axis=-1, keepdims=True) + EPS) * fg_ref[...]
        y_ref[...] = y


def _peer_dense(sc, n2, u, vt, p, c, q, r, h, fg):
    tb, eb = DENSE_TB, DENSE_EB
    keys = eb // PEER_NKEYS
    lane_tiles = tb // V7X_LANES
    blk = (tb * D_MODEL + eb * D_MODEL * (1 + 2) + 2 * PEER_HEADS * keys * tb * 4
           + 2 * PEER_HEADS * PEER_NKEYS * tb * 2 + 2 * tb * D_MODEL * 4)
    scratch = D_MODEL * tb * 4 + eb * tb * 2
    return pl.pallas_call(
        _dense_kernel,
        grid=(TOKENS // tb, PEER_EXPERTS // eb),
        in_specs=[
            pl.BlockSpec(memory_space=pltpu.SMEM),
            pl.BlockSpec((D_MODEL, tb), lambda t, e: (0, t)),
            pl.BlockSpec((eb, D_MODEL), lambda t, e: (e, 0)),
            pl.BlockSpec((D_MODEL, eb), lambda t, e: (0, e)),
            pl.BlockSpec((PEER_HEADS, lane_tiles, keys, V7X_LANES), lambda t, e: (0, t, e, 0)),
            pl.BlockSpec((PEER_HEADS, lane_tiles, keys, V7X_LANES), lambda t, e: (0, t, e, 0)),
            pl.BlockSpec((PEER_HEADS, PEER_NKEYS, tb), lambda t, e: (0, 0, t)),
            pl.BlockSpec((PEER_HEADS, PEER_NKEYS, tb), lambda t, e: (0, 0, t)),
            pl.BlockSpec((tb, D_MODEL), lambda t, e: (t, 0)),
            pl.BlockSpec((1, D_MODEL), lambda t, e: (0, 0)),
        ],
        out_specs=pl.BlockSpec((tb, D_MODEL), lambda t, e: (t, 0)),
        out_shape=jax.ShapeDtypeStruct((TOKENS, D_MODEL), F32),
        scratch_shapes=[pltpu.VMEM((D_MODEL, tb), F32), pltpu.VMEM((eb, tb), BF16)],
        compiler_params=pltpu.CompilerParams(
            dimension_semantics=("arbitrary", "arbitrary"),
            vmem_limit_bytes=_vmem_limit(blk + scratch // 2)),
        name="peer_dense",
    )(sc, n2, u, vt, p, c, q, r, h, fg)


def kernel(x, norm1_g, w_in, w_gate_up, b_gate, gla_norm_g, sgu_ln_g, sgu_ln_b, sgu_w, sgu_b,
           w_branch_a, w_branch_b, w_out, norm2_g, peer_wq, peer_k1, peer_k2, peer_u, peer_v,
           final_g):
    assert x.shape == (BATCH, SEQ, D_MODEL) and w_in.shape[0] == 1
    row = lambda a: a.reshape(1, -1).astype(F32)

    w = w_in[0]
    o_alr = 4 * GLA_W
    o_su = o_alr + GLA_LOWRANK
    o_sv = o_su + SGU_W
    o_ga = o_sv + SGU_W
    o_gb = o_ga + D_MODEL
    alr_pad = jnp.pad(w[:, o_alr:o_su], ((0, 0), (0, V7X_LANES - GLA_LOWRANK)))
    win = jnp.concatenate([w[:, :o_alr], w[:, o_su:o_sv], w[:, o_sv:o_ga], w[:, o_ga:o_gb],
                           w[:, o_gb:], alr_pad], axis=1).astype(BF16)
    wa2 = jnp.pad(w_gate_up[0], ((0, V7X_LANES - GLA_LOWRANK), (0, 0))).astype(BF16)
    sbb = jnp.broadcast_to(sgu_b[0][:, :, None], (SGU_GROUPS, SGU_BLOCK, SGU_BLOCK)).astype(F32)

    s_n = _pow2_scale(math.sqrt(D_MODEL) * jnp.max(jnp.abs(norm2_g[0])))
    s_u = _pow2_scale(jnp.max(jnp.abs(peer_u[0])))

    h, n2, n2q = _mixers(s_n.reshape(1).astype(F32), x, row(norm1_g[0]), win, wa2, row(b_gate[0]),
                         row(gla_norm_g[0]), row(sgu_ln_g[0]), row(sgu_ln_b[0]), sgu_w[0], sbb,
                         w_branch_a[0].astype(BF16), w_branch_b[0].astype(BF16),
                         w_out[0].astype(BF16), row(norm2_g[0]))
    h = h.reshape(TOKENS, D_MODEL)

    wfold = _peer_fold(jnp.stack([peer_k1[0], peer_k2[0]]), peer_wq[0])
    hk = PEER_HEADS * PEER_NKEYS
    p, c, q, r = _peer_select(n2, wfold[:hk], wfold[hk:])

    inv = 1.0 / (s_u * s_n)
    c_tanh = math.sqrt(2.0 / math.pi)
    sc = jnp.stack([c_tanh * inv, c_tanh * 0.044715 * inv * inv * inv, 0.5 * inv]).astype(F32)
    u = (peer_u[0] * s_u).astype(FP8)
    vt = peer_v[0].T.astype(BF16)
    y = _peer_dense(sc, n2q, u, vt, p, c, q, r, h, row(final_g))
    return y.reshape(BATCH, SEQ, D_MODEL)
```
